```python
import math
import jax, jax.numpy as jnp
from jax import lax
import numpy as np

D_MODEL = 4096
BATCH = 32
SEQ = 256
DEPTH = 4
DEC_BATCH = 2
DEC_SEQ = 1024
PAST_LEN = 512

GRID_W = 64
MIX_W = D_MODEL
S5_W = D_MODEL // 4
S5_H = 16
S5_G = S5_W // S5_H
S5_P = 64
ML_W = D_MODEL // 2
ML_NH = 4
ML_DV = ML_W // ML_NH
ML_DQK = ML_DV // 2
ML_QK_W = ML_NH * ML_DQK
ML_GATE_W = 4 * ML_NH
ML_CHUNK = 64
LRU_W = D_MODEL // 4
LRU_NB = 16
LRU_BS = LRU_W // LRU_NB
CONV_W = 4
CONV_PAD = (2, 1)
LRU_C = 8.0
D_FF = 4 * D_MODEL
EPS = 1e-6
IN_SIZES = (S5_W, ML_QK_W, ML_QK_W, ML_W, ML_W, ML_GATE_W, LRU_W, LRU_W)
N_IN = S5_W + 2 * ML_QK_W + 2 * ML_W + ML_GATE_W + 2 * LRU_W

kernel_name = "hybrid_s5_mlstm_rglru_flow_step"


def rmsnorm(x, g):
    xf = x.astype(jnp.float32)
    y = xf * lax.rsqrt(jnp.mean(xf * xf, axis=-1, keepdims=True) + EPS) * g.astype(jnp.float32)
    return y.astype(x.dtype)


def linear_scan(a, b, h0):
    b = b.at[:, 0].add(a[:, 0] * h0)

    def combine(lhs, rhs):
        return lhs[0] * rhs[0], rhs[0] * lhs[1] + rhs[1]

    return lax.associative_scan(combine, (a, b), axis=1)[1]


def directional_scan(a, b, h0, reverse):
    if reverse:
        h = jnp.flip(linear_scan(jnp.flip(a, 1), jnp.flip(b, 1), h0), 1)
        return h, h[:, 0]
    h = linear_scan(a, b, h0)
    return h, h[:, -1]


def s5_mixer(u, h0, lam_re, lam_im, log_dt, b_re, b_im, c_re, c_im, d, w_glu, b_glu):
    f32 = jnp.float32
    bsz, length, _ = u.shape
    uf = u.astype(f32).reshape(bsz, length, S5_G, S5_H)
    lam = lax.complex(lam_re.astype(f32), lam_im.astype(f32))
    dt = jnp.exp(log_dt.astype(f32))[..., None]
    lam_bar = jnp.exp(lam * dt)
    b_cplx = lax.complex(b_re.astype(f32), b_im.astype(f32))
    b_bar = ((lam_bar - 1.0) / lam)[..., None] * b_cplx
    y = d.astype(f32) * uf
    finals = []
    for dr in range(2):
        bu = lax.complex(jnp.einsum('blgh,gph->blgp', uf, b_bar[dr].real),
                         jnp.einsum('blgh,gph->blgp', uf, b_bar[dr].imag))
        a = jnp.broadcast_to(lam_bar[dr], bu.shape)
        h, fin = directional_scan(a, bu, h0[:, dr], reverse=(dr == 1))
        y = y + jnp.einsum('blgp,ghp->blgh', h.real, c_re[dr].astype(f32)) \
              - jnp.einsum('blgp,ghp->blgh', h.imag, c_im[dr].astype(f32))
        finals.append(fin)
    y = jax.nn.gelu(y.reshape(bsz, length, S5_W))
    out = y * jax.nn.sigmoid(y @ w_glu.astype(f32) + b_glu.astype(f32))
    return out.astype(u.dtype), jnp.stack(finals, 1)


def mlstm_chunkwise(q, k, v, i_pre, logf, state0):
    bsz, nh, length, _ = q.shape
    nc = length // ML_CHUNK

    def to_chunks(t):
        t = t.reshape(bsz, nh, nc, ML_CHUNK, *t.shape[3:])
        return jnp.moveaxis(t, 2, 0)

    xs = (to_chunks(q), to_chunks(k), to_chunks(v), to_chunks(i_pre), to_chunks(logf))
    causal = jnp.tril(jnp.ones((ML_CHUNK, ML_CHUNK), dtype=bool))

    def step(carry, chunk):
        c_st, n_st, m_st = carry
        qc, kc, vc, ic, fc = chunk
        bcum = jnp.cumsum(fc, axis=-1)
        dmat = bcum[..., :, None] - bcum[..., None, :] + ic[..., None, :]
        dmat = jnp.where(causal, dmat, -jnp.inf)
        inter = bcum + m_st[..., None]
        m_t = jnp.maximum(inter, jnp.max(dmat, axis=-1))
        scores = jnp.einsum('bhtd,bhsd->bhts', qc, kc) * jnp.exp(dmat - m_t[..., None])
        w_inter = jnp.exp(inter - m_t)
        num = jnp.einsum('bhts,bhsv->bhtv', scores, vc) \
            + w_inter[..., None] * jnp.einsum('bhvd,bhtd->bhtv', c_st, qc)
        den = jnp.sum(scores, axis=-1) + w_inter * jnp.einsum('bhd,bhtd->bht', n_st, qc)
        h = num / jnp.maximum(jnp.abs(den), jnp.exp(-m_t))[..., None]
        b_last = bcum[..., -1]
        g = b_last[..., None] - bcum + ic
        m_new = jnp.maximum(b_last + m_st, jnp.max(g, axis=-1))
        wg = jnp.exp(g - m_new[..., None])
        decay = jnp.exp(b_last + m_st - m_new)
        c_new = decay[..., None, None] * c_st + jnp.einsum('bhsv,bhsd->bhvd', vc * wg[..., None], kc)
        n_new = decay[..., None] * n_st + jnp.einsum('bhs,bhsd->bhd', wg, kc)
        return (c_new, n_new, m_new), h

    final, hs = lax.scan(step, state0, xs)
    h = jnp.moveaxis(hs, 0, 2).reshape(bsz, nh, length, v.shape[-1])
    return h, final


def mlstm_mixer(q, k, v, o, gates, state0, b_gate, norm_g):
    f32 = jnp.float32
    bsz, length, _ = q.shape

    def heads(t, dim):
        return t.astype(f32).reshape(bsz, length, ML_NH, dim).transpose(0, 2, 1, 3)

    qh = heads(q, ML_DQK) * (ML_DQK ** -0.5)
    kh = heads(k, ML_DQK)
    vh = heads(v, ML_DV)
    g = (gates.astype(f32).reshape(bsz, length, 4, ML_NH) + b_gate.astype(f32)).transpose(0, 2, 3, 1)
    c0, n0, m0 = state0
    h_sum = jnp.zeros_like(vh)
    fins_c, fins_n, fins_m = [], [], []
    for dr in range(2):
        i_pre = g[:, 2 * dr]
        logf = jax.nn.log_sigmoid(g[:, 2 * dr + 1])
        st = (c0[:, dr], n0[:, dr], m0[:, dr])
        if dr == 1:
            h, (cf, nf, mf) = mlstm_chunkwise(jnp.flip(qh, 2), jnp.flip(kh, 2), jnp.flip(vh, 2),
                                              jnp.flip(i_pre, 2), jnp.flip(logf, 2), st)
            h = jnp.flip(h, 2)
        else:
            h, (cf, nf, mf) = mlstm_chunkwise(qh, kh, vh, i_pre, logf, st)
        h_sum = h_sum + h
        fins_c.append(cf)
        fins_n.append(nf)
        fins_m.append(mf)
    hn = h_sum * lax.rsqrt(jnp.mean(h_sum * h_sum, axis=-1, keepdims=True) + EPS)
    hn = hn.transpose(0, 2, 1, 3).reshape(bsz, length, ML_W) * norm_g.astype(f32)
    out = jax.nn.sigmoid(o.astype(f32)) * hn
    return out.astype(q.dtype), (jnp.stack(fins_c, 1), jnp.stack(fins_n, 1), jnp.stack(fins_m, 1))


def raster_to_columns(t):
    bsz, length, ch = t.shape
    rows = length // GRID_W
    return t.reshape(bsz, rows, GRID_W, ch).transpose(0, 2, 1, 3).reshape(bsz, length, ch)


def columns_to_raster(t):
    bsz, length, ch = t.shape
    rows = length // GRID_W
    return t.reshape(bsz, GRID_W, rows, ch).transpose(0, 2, 1, 3).reshape(bsz, length, ch)


def rglru_mixer(xb, gb, h0, conv_w, conv_b, wa, ba, wx, bx, lam, latent):
    f32 = jnp.float32
    if latent:
        xb = raster_to_columns(xb)
    bsz, length, _ = xb.shape
    xc = lax.conv_general_dilated(xb, conv_w[:, None, :].astype(xb.dtype), (1,), [CONV_PAD],
                                  dimension_numbers=('NWC', 'WIO', 'NWC'),
                                  feature_group_count=LRU_W) + conv_b
    xf = xc.astype(f32)
    blocks = xf.reshape(bsz, length, LRU_NB, LRU_BS)
    y = jnp.zeros_like(xf)
    finals = []
    for dr in range(2):
        r = jax.nn.sigmoid(jnp.einsum('blni,nio->blno', blocks, wa[dr].astype(f32)).reshape(bsz, length, LRU_W)
                           + ba[dr].astype(f32))
        i = jax.nn.sigmoid(jnp.einsum('blni,nio->blno', blocks, wx[dr].astype(f32)).reshape(bsz, length, LRU_W)
                           + bx[dr].astype(f32))
        log_a = -LRU_C * r * jax.nn.softplus(-lam[dr].astype(f32))
        a = jnp.exp(log_a)
        b = jnp.sqrt(-jnp.expm1(2.0 * log_a)) * (i * xf)
        h, fin = directional_scan(a, b, h0[:, dr], reverse=(dr == 1))
        y = y + h
        finals.append(fin)
    if latent:
        y = columns_to_raster(y)
    out = y * jax.nn.gelu(gb.astype(f32))
    return out.astype(gb.dtype), jnp.stack(finals, 1)


def trunk_layer(x, cond, states, p, latent):
    (w_ada, b_ada, gains, w_in, w_out, lam_re, lam_im, log_dt, b_re, b_im, c_re, c_im, s5_d, w_glu, b_glu,
     ml_b_gate, ml_norm, conv_w, conv_b, wa, ba, wx, bx, lam, w_ff1, w_ff2) = p
    s5_h0, ml_c0, ml_n0, ml_m0, lru_h0 = states
    mod = (jax.nn.silu(cond) @ w_ada + b_ada).reshape(cond.shape[0], 6, 1, D_MODEL)
    shift1, scale1, gate1, shift2, scale2, gate2 = (mod[:, j] for j in range(6))
    hn = rmsnorm(x, gains[0]) * (1 + scale1) + shift1
    z = hn @ w_in
    split_points = np.cumsum(IN_SIZES)[:-1].tolist()
    u, q, k, v, o, gates, xb, gb = jnp.split(z, split_points, axis=-1)
    y_s5, s5_fin = s5_mixer(u, s5_h0, lam_re, lam_im, log_dt, b_re, b_im, c_re, c_im, s5_d, w_glu, b_glu)
    y_ml, ml_fin = mlstm_mixer(q, k, v, o, gates, (ml_c0, ml_n0, ml_m0), ml_b_gate, ml_norm)
    y_lru, lru_fin = rglru_mixer(xb, gb, lru_h0, conv_w, conv_b, wa, ba, wx, bx, lam, latent)
    mix = jnp.concatenate([y_s5, y_ml, y_lru], axis=-1) @ w_out
    x = x + gate1 * rmsnorm(mix, gains[1])
    hn = rmsnorm(x, gains[2]) * (1 + scale2) + shift2
    ff = jnp.square(jax.nn.relu(hn @ w_ff1)) @ w_ff2
    x = x + gate2 * rmsnorm(ff, gains[3])
    return x, (s5_fin, ml_fin[0], ml_fin[1], ml_fin[2], lru_fin)


def setup_inputs(seed: int = 0) -> dict:
    key = jax.random.key(seed)
    ks = iter(jax.random.split(key, 48))
    f32 = jnp.float32

    def nrm(shape, scale=1.0):
        return jax.random.normal(next(ks), shape, f32) * scale

    def unif(shape, lo, hi):
        return jax.random.uniform(next(ks), shape, f32, lo, hi)

    f_bias = jnp.linspace(3.0, 6.0, ML_NH, dtype=f32)
    zeros_nh = jnp.zeros((ML_NH,), f32)
    gate_base = jnp.stack([zeros_nh, f_bias, zeros_nh, f_bias])
    a_target = unif((DEPTH, 2, LRU_W), 0.9, 0.999) ** (1.0 / LRU_C)
    return {
        'x_prompt': nrm((BATCH, SEQ, D_MODEL)),
        'x_sample': nrm((DEC_BATCH, DEC_SEQ, D_MODEL)),
        'c': nrm((DEC_BATCH, D_MODEL)),
        'state_s5': nrm((DEC_BATCH, DEPTH, 2, S5_G, S5_P, 2), 0.1),
        'state_mlstm_c': nrm((DEC_BATCH, DEPTH, 2, ML_NH, ML_DV, ML_DQK), 0.5),
        'state_mlstm_n': nrm((DEC_BATCH, DEPTH, 2, ML_NH, ML_DQK), 0.5),
        'state_mlstm_m': nrm((DEC_BATCH, DEPTH, 2, ML_NH)),
        'state_lru': nrm((DEC_BATCH, DEPTH, 2, LRU_W), 0.5),
        'c_ctx': nrm((D_MODEL,)),
        'w_ada': nrm((DEPTH, D_MODEL, 6 * D_MODEL), D_MODEL ** -0.5),
        'b_ada': nrm((DEPTH, 6 * D_MODEL), 0.02),
        'norm_gains': 1.0 + nrm((DEPTH, 4, D_MODEL), 0.02),
        'w_in': nrm((DEPTH, D_MODEL, N_IN), D_MODEL ** -0.5),
        'w_out': nrm((DEPTH, MIX_W, D_MODEL), MIX_W ** -0.5),
        's5_lam_re': -0.5 + nrm((DEPTH, 2, S5_G, S5_P), 0.01),
        's5_lam_im': jnp.pi * jnp.arange(S5_P, dtype=f32) + nrm((DEPTH, 2, S5_G, S5_P), 0.01),
        's5_log_dt': unif((DEPTH, 2, S5_G), math.log(1e-3), math.log(1e-1)),
        's5_b_re': nrm((DEPTH, 2, S5_G, S5_P, S5_H), (2 * S5_H) ** -0.5),
        's5_b_im': nrm((DEPTH, 2, S5_G, S5_P, S5_H), (2 * S5_H) ** -0.5),
        's5_c_re': nrm((DEPTH, 2, S5_G, S5_H, S5_P), S5_P ** -0.5),
        's5_c_im': nrm((DEPTH, 2, S5_G, S5_H, S5_P), S5_P ** -0.5),
        's5_d': nrm((DEPTH, S5_G, S5_H)),
        's5_w_glu': nrm((DEPTH, S5_W, S5_W), S5_W ** -0.5),
        's5_b_glu': nrm((DEPTH, S5_W), 0.02),
        'ml_b_gate': gate_base[None] + nrm((DEPTH, 4, ML_NH), 0.1),
        'ml_norm': 1.0 + nrm((DEPTH, ML_W), 0.02),
        'lru_conv_w': nrm((DEPTH, CONV_W, LRU_W), CONV_W ** -0.5),
        'lru_conv_b': nrm((DEPTH, LRU_W), 0.02),
        'lru_wa': nrm((DEPTH, 2, LRU_NB, LRU_BS, LRU_BS), LRU_BS ** -0.5),
        'lru_ba': nrm((DEPTH, 2, LRU_W), 0.02),
        'lru_wx': nrm((DEPTH, 2, LRU_NB, LRU_BS, LRU_BS), LRU_BS ** -0.5),
        'lru_bx': nrm((DEPTH, 2, LRU_W), 0.02),
        'lru_lam': jnp.log(a_target) - jnp.log1p(-a_target),
        'w_ff1': nrm((DEPTH, D_MODEL, D_FF), D_MODEL ** -0.5),
        'w_ff2': nrm((DEPTH, D_FF, D_MODEL), D_FF ** -0.5),
    }


def reference(x_prompt, x_sample, c, state_s5, state_mlstm_c, state_mlstm_n, state_mlstm_m, state_lru,
              c_ctx, w_ada, b_ada, norm_gains, w_in, w_out, s5_lam_re, s5_lam_im, s5_log_dt, s5_b_re, s5_b_im,
              s5_c_re, s5_c_im, s5_d, s5_w_glu, s5_b_glu, ml_b_gate, ml_norm, lru_conv_w, lru_conv_b,
              lru_wa, lru_ba, lru_wx, lru_bx, lru_lam, w_ff1, w_ff2):
    f32 = jnp.float32
    stacked = (w_ada, b_ada, norm_gains, w_in, w_out, s5_lam_re, s5_lam_im, s5_log_dt, s5_b_re, s5_b_im,
               s5_c_re, s5_c_im, s5_d, s5_w_glu, s5_b_glu, ml_b_gate, ml_norm, lru_conv_w, lru_conv_b,
               lru_wa, lru_ba, lru_wx, lru_bx, lru_lam, w_ff1, w_ff2)

    bp = x_prompt.shape[0]
    zero_states = (jnp.zeros((bp, 2, S5_G, S5_P), jnp.complex64),
                   jnp.zeros((bp, 2, ML_NH, ML_DV, ML_DQK), f32),
                   jnp.zeros((bp, 2, ML_NH, ML_DQK), f32),
                   jnp.zeros((bp, 2, ML_NH), f32),
                   jnp.zeros((bp, 2, LRU_W), f32))
    h = x_prompt
    s5_l, mc_l, mn_l, mm_l, lru_l = [], [], [], [], []
    for layer in range(DEPTH):
        h, (s5f, mcf, mnf, mmf, lruf) = trunk_layer(h, c_ctx[None, :], zero_states,
                                                    tuple(t[layer] for t in stacked), latent=False)
        s5_l.append(jnp.stack([s5f.real, s5f.imag], axis=-1))
        mc_l.append(mcf)
        mn_l.append(mnf)
        mm_l.append(mmf)
        lru_l.append(lruf)
    y_prompt = h
    sdt = x_prompt.dtype
    new_state_s5 = jnp.stack(s5_l, 1).astype(sdt)
    new_state_mlstm_c = jnp.stack(mc_l, 1).astype(sdt)
    new_state_mlstm_n = jnp.stack(mn_l, 1).astype(sdt)
    new_state_mlstm_m = jnp.stack(mm_l, 1).astype(sdt)
    new_state_lru = jnp.stack(lru_l, 1).astype(sdt)

    h = x_sample
    for layer in range(DEPTH):
        st = (lax.complex(state_s5[:, layer, ..., 0].astype(f32), state_s5[:, layer, ..., 1].astype(f32)),
              state_mlstm_c[:, layer].astype(f32),
              state_mlstm_n[:, layer].astype(f32),
              state_mlstm_m[:, layer].astype(f32),
              state_lru[:, layer].astype(f32))
        h, _ = trunk_layer(h, c, st, tuple(t[layer] for t in stacked), latent=True)
    y_sample = h

    return (y_prompt, y_sample, new_state_s5, new_state_mlstm_c, new_state_mlstm_n, new_state_mlstm_m, new_state_lru)
```

```python
import functools

import jax
import jax.numpy as jnp
from jax import lax
from jax.experimental import pallas as pl
from jax.experimental.pallas import tpu as pltpu

F32 = jnp.float32
BF16 = jnp.bfloat16

EPS = 1e-6
GRID_W = 64
LRU_C = 8.0
CONV_LEFT = 2
SUBLANES = 8
MXU_DIM = 256
LANES = 128
SCAN_STEPS = 32
ML_CHUNK = 256
VMEM_LIMIT = 56 * 1024 * 1024
NEG_INF = float("-inf")


def _cparams(sem):
    return pltpu.CompilerParams(dimension_semantics=sem, vmem_limit_bytes=VMEM_LIMIT)


def _tile(n, target):
    t = min(n, target)
    while n % t:
        t -= 1
    return t


def _ada_kernel(c_ref, w_ref, b_ref, o_ref):
    c = c_ref[...]
    s = (c * jax.nn.sigmoid(c)).astype(BF16)
    o_ref[0] = jnp.dot(s, w_ref[0].astype(BF16), preferred_element_type=F32) + b_ref[0]


def _ada_mod(cond, w_ada, b_ada):
    depth, d, n = w_ada.shape
    tn = _tile(n, 1024)
    return pl.pallas_call(
        _ada_kernel,
        grid=(depth, n // tn),
        in_specs=[pl.BlockSpec((SUBLANES, d), lambda l, j: (0, 0)),
                  pl.BlockSpec((1, d, tn), lambda l, j: (l, 0, j)),
                  pl.BlockSpec((1, 1, tn), lambda l, j: (l, 0, j))],
        out_specs=pl.BlockSpec((1, SUBLANES, tn), lambda l, j: (l, 0, j)),
        out_shape=jax.ShapeDtypeStruct((depth, SUBLANES, n), F32),
        compiler_params=_cparams(("arbitrary", "arbitrary")),
    )(cond, w_ada, b_ada.reshape(depth, 1, n))


def _rms(x, g):
    return x * lax.rsqrt(jnp.mean(x * x, axis=-1, keepdims=True) + EPS) * g


def _prenorm_kernel(x_ref, mod_ref, g_ref, h_ref, *, shift_idx, scale_idx):
    m = mod_ref[0]
    hn = _rms(x_ref[...], g_ref[...])
    h_ref[...] = (hn * (1.0 + m[scale_idx:scale_idx + 1]) + m[shift_idx:shift_idx + 1]).astype(h_ref.dtype)


def _resnorm_kernel(x_ref, y_ref, mod_ref, modn_ref, gp_ref, gn_ref, xo_ref, *rest,
                    gate_idx, shift_idx, scale_idx):
    m = mod_ref[0]
    x = x_ref[...] + m[gate_idx:gate_idx + 1] * _rms(y_ref[...], gp_ref[...])
    xo_ref[...] = x
    if rest:
        mn = modn_ref[0]
        hn = _rms(x, gn_ref[...])
        rest[0][...] = (hn * (1.0 + mn[scale_idx:scale_idx + 1]) + mn[shift_idx:shift_idx + 1]).astype(BF16)


def _gcd(a, b):
    while b:
        a, b = b, a % b
    return a


def _seg_map(tm, m_ctx, l_lat):
    nct = m_ctx // tm
    per = l_lat // tm

    def seg(i):
        return jnp.where(i < nct, 0, 1 + (i - nct) // per)

    return seg


def _prenorm(x, mod_l, gain, m_ctx, l_lat, shift_idx, scale_idx):
    m, d = x.shape
    tm = _tile(_gcd(m_ctx, l_lat), 256)
    seg = _seg_map(tm, m_ctx, l_lat)
    return pl.pallas_call(
        functools.partial(_prenorm_kernel, shift_idx=shift_idx, scale_idx=scale_idx),
        grid=(m // tm,),
        in_specs=[pl.BlockSpec((tm, d), lambda i: (i, 0)),
                  pl.BlockSpec((1, 6, d), lambda i: (seg(i), 0, 0)),
                  pl.BlockSpec((1, d), lambda i: (0, 0))],
        out_specs=pl.BlockSpec((tm, d), lambda i: (i, 0)),
        out_shape=jax.ShapeDtypeStruct((m, d), BF16),
        compiler_params=_cparams(("arbitrary",)),
    )(x, mod_l, gain.reshape(1, d))


def _resnorm(x, y, mod_l, mod_next, g_post, g_next, m_ctx, l_lat, gate_idx, shift_idx, scale_idx, with_next):
    m, d = x.shape
    tm = _tile(_gcd(m_ctx, l_lat), 256)
    seg = _seg_map(tm, m_ctx, l_lat)
    row = pl.BlockSpec((tm, d), lambda i: (i, 0))
    modspec = pl.BlockSpec((1, 6, d), lambda i: (seg(i), 0, 0))
    vec = pl.BlockSpec((1, d), lambda i: (0, 0))
    out_shape = [jax.ShapeDtypeStruct((m, d), F32)]
    out_specs = [row]
    if with_next:
        out_shape.append(jax.ShapeDtypeStruct((m, d), BF16))
        out_specs.append(row)
    outs = pl.pallas_call(
        functools.partial(_resnorm_kernel, gate_idx=gate_idx, shift_idx=shift_idx, scale_idx=scale_idx),
        grid=(m // tm,),
        in_specs=[row, row, modspec, modspec, vec, vec],
        out_specs=out_specs,
        out_shape=out_shape,
        compiler_params=_cparams(("arbitrary",)),
    )(x, y, mod_l, mod_next, g_post.reshape(1, d), g_next.reshape(1, d))
    return outs if with_next else (outs[0], None)


def _mm_kernel(a_ref, w_ref, o_ref, *, relu2):
    acc = jnp.dot(a_ref[...], w_ref[...], preferred_element_type=F32)
    if relu2:
        acc = jnp.square(jnp.maximum(acc, 0.0))
    o_ref[...] = acc.astype(o_ref.dtype)


def _mm_acc_kernel(a_ref, w_ref, o_ref, acc_ref):
    k = pl.program_id(2)

    @pl.when(k == 0)
    def _():
        acc_ref[...] = jnp.zeros_like(acc_ref)

    acc_ref[...] += jnp.dot(a_ref[...], w_ref[...], preferred_element_type=F32)

    @pl.when(k == pl.num_programs(2) - 1)
    def _():
        o_ref[...] = acc_ref[...].astype(o_ref.dtype)


def _matmul(a, w, out_dtype=F32, relu2=False):
    m, k = a.shape
    n = w.shape[1]
    tm = _tile(m, 1024)
    tn = _tile(n, 1024)
    tk = _tile(k, 4096)
    if tk == k:
        return pl.pallas_call(
            functools.partial(_mm_kernel, relu2=relu2),
            grid=(n // tn, m // tm),
            in_specs=[pl.BlockSpec((tm, k), lambda j, i: (i, 0)),
                      pl.BlockSpec((k, tn), lambda j, i: (0, j))],
            out_specs=pl.BlockSpec((tm, tn), lambda j, i: (i, j)),
            out_shape=jax.ShapeDtypeStruct((m, n), out_dtype),
            compiler_params=_cparams(("arbitrary", "arbitrary")),
        )(a, w)
    assert not relu2
    return pl.pallas_call(
        _mm_acc_kernel,
        grid=(n // tn, m // tm, k // tk),
        in_specs=[pl.BlockSpec((tm, tk), lambda j, i, kk: (i, kk)),
                  pl.BlockSpec((tk, tn), lambda j, i, kk: (kk, j))],
        out_specs=pl.BlockSpec((tm, tn), lambda j, i, kk: (i, j)),
        out_shape=jax.ShapeDtypeStruct((m, n), out_dtype),
        scratch_shapes=[pltpu.VMEM((tm, tn), F32)],
        compiler_params=_cparams(("arbitrary", "arbitrary", "arbitrary")),
    )(a, w)


def _s5_kernel(u_ref, bd_ref, cd_ref, lr_ref, li_ref, h0_ref, y_ref, fin_ref, bu_ref, hc_ref,
               *, steps, ncb, cbw, gp, chunk):
    dr = pl.program_id(1)
    j = pl.program_id(2)
    sw = 2 * gp

    @pl.when(j == 0)
    def _():
        hc_ref[...] = h0_ref[0, 0]

    u = u_ref[0].astype(BF16)
    for cb in range(ncb):
        bu_ref[:, cb * sw:(cb + 1) * sw] = jnp.dot(
            u[:, cb * cbw:(cb + 1) * cbw], bd_ref[0, cb], preferred_element_type=F32)

    for cb in range(ncb):
        for off in range(0, gp, chunk):
            re = pl.ds(cb * sw + off, chunk)
            im = pl.ds(cb * sw + gp + off, chunk)
            lam = pl.ds(cb * gp + off, chunk)
            lr = lr_ref[0, :, lam]
            li = li_ref[0, :, lam]

            def body(t, carry, re=re, im=im, lr=lr, li=li):
                hr, hi = carry
                tt = t + dr * (steps - 1 - 2 * t)
                rows = pl.ds(pl.multiple_of(tt * SUBLANES, SUBLANES), SUBLANES)
                nr = lr * hr - li * hi + bu_ref[rows, re]
                ni = lr * hi + li * hr + bu_ref[rows, im]
                bu_ref[rows, re] = nr
                bu_ref[rows, im] = ni
                return nr, ni

            hr, hi = lax.fori_loop(0, steps, body, (hc_ref[:, re], hc_ref[:, im]), unroll=4)
            hc_ref[:, re] = hr
            hc_ref[:, im] = hi

    for cb in range(ncb):
        y_ref[0, 0, :, cb * cbw:(cb + 1) * cbw] = jnp.dot(
            bu_ref[:, cb * sw:(cb + 1) * sw].astype(BF16), cd_ref[0, cb], preferred_element_type=F32)
    fin_ref[0, 0] = hc_ref[...]


def _s5_scan(u_t, bd, cd, lam_re, lam_im, h0, steps_total):
    ng, rows_total, w = u_t.shape
    ncb, cbw, sw = bd.shape[1], bd.shape[2], bd.shape[3]
    gp = sw // 2
    sl = ncb * gp
    steps = _tile(steps_total, SCAN_STEPS)
    nt = steps_total // steps
    rows = steps * SUBLANES
    chunk = _tile(gp, 512)

    def tb(dr, j):
        return j + dr * (nt - 1 - 2 * j)

    return pl.pallas_call(
        functools.partial(_s5_kernel, steps=steps, ncb=ncb, cbw=cbw, gp=gp, chunk=chunk),
        grid=(ng, 2, nt),
        in_specs=[pl.BlockSpec((1, rows, w), lambda g, dr, j: (g, tb(dr, j), 0)),
                  pl.BlockSpec((1, ncb, cbw, sw), lambda g, dr, j: (dr, 0, 0, 0)),
                  pl.BlockSpec((1, ncb, sw, cbw), lambda g, dr, j: (dr, 0, 0, 0)),
                  pl.BlockSpec((1, SUBLANES, sl), lambda g, dr, j: (dr, 0, 0)),
                  pl.BlockSpec((1, SUBLANES, sl), lambda g, dr, j: (dr, 0, 0)),
                  pl.BlockSpec((1, 1, SUBLANES, 2 * sl), lambda g, dr, j: (g, dr, 0, 0))],
        out_specs=[pl.BlockSpec((1, 1, rows, w), lambda g, dr, j: (dr, g, tb(dr, j), 0)),
                   pl.BlockSpec((1, 1, SUBLANES, 2 * sl), lambda g, dr, j: (g, dr, 0, 0))],
        out_shape=[jax.ShapeDtypeStruct((2, ng, rows_total, w), F32),
                   jax.ShapeDtypeStruct((ng, 2, SUBLANES, 2 * sl), F32)],
        scratch_shapes=[pltpu.VMEM((rows, 2 * sl), F32), pltpu.VMEM((SUBLANES, 2 * sl), F32)],
        compiler_params=_cparams(("arbitrary", "arbitrary", "arbitrary")),
    )(u_t, bd, cd, lam_re, lam_im, h0)


def _gelu_tanh(x):
    return 0.5 * x * (1.0 + jnp.tanh(0.7978845608028654 * (x + 0.044715 * (x * x * x))))


def _s5_post_kernel(yd_ref, u_ref, d_ref, w_ref, b_ref, o_ref):
    y = yd_ref[0] + yd_ref[1] + d_ref[...] * u_ref[...]
    y = _gelu_tanh(y)
    gl = jnp.dot(y.astype(BF16), w_ref[...], preferred_element_type=F32) + b_ref[...]
    o_ref[...] = (y * jax.nn.sigmoid(gl)).astype(o_ref.dtype)


def _s5_post(yd, u_t, d_row, w_glu, b_glu):
    _, m, w = yd.shape
    tm = _tile(m, 512)
    return pl.pallas_call(
        _s5_post_kernel,
        grid=(m // tm,),
        in_specs=[pl.BlockSpec((2, tm, w), lambda i: (0, i, 0)),
                  pl.BlockSpec((tm, w), lambda i: (i, 0)),
                  pl.BlockSpec((1, w), lambda i: (0, 0)),
                  pl.BlockSpec((w, w), lambda i: (0, 0)),
                  pl.BlockSpec((1, w), lambda i: (0, 0))],
        out_specs=pl.BlockSpec((tm, w), lambda i: (i, 0)),
        out_shape=jax.ShapeDtypeStruct((m, w), BF16),
        compiler_params=_cparams(("arbitrary",)),
    )(yd, u_t, d_row, w_glu, b_glu)


def _softplus(x):
    return jnp.maximum(x, 0.0) + jnp.log1p(jnp.exp(-jnp.abs(x)))


def _lru_kernel(x_ref, xp_ref, xn_ref, cw_ref, cb_ref, wa_ref, wx_ref, ba_ref, bx_ref, lam_ref, h0_ref,
                y_ref, fin_ref, a_ref, b_ref, hc_ref, *, steps, nt, ntile, tw):
    dr = pl.program_id(1)
    j = pl.program_id(2)
    tblk = j + dr * (nt - 1 - 2 * j)
    rows = steps * SUBLANES

    @pl.when(j == 0)
    def _():
        hc_ref[...] = h0_ref[0, 0]

    prev = jnp.where(tblk > 0, xp_ref[0], 0.0)
    nxt = jnp.where(tblk < nt - 1, xn_ref[0], 0.0)
    ext = jnp.concatenate([prev, x_ref[0], nxt], axis=0)
    cw = cw_ref[...]
    xc = cb_ref[...] + sum(cw[q:q + 1] * ext[q * SUBLANES:q * SUBLANES + rows] for q in range(cw.shape[0]))
    xcb = xc.astype(BF16)
    sp = _softplus(-lam_ref[0])
    for t in range(ntile):
        cs = slice(t * tw, (t + 1) * tw)
        r = jax.nn.sigmoid(jnp.dot(xcb[:, cs], wa_ref[0, t], preferred_element_type=F32) + ba_ref[0, :, cs])
        i = jax.nn.sigmoid(jnp.dot(xcb[:, cs], wx_ref[0, t], preferred_element_type=F32) + bx_ref[0, :, cs])
        log_a = (-LRU_C) * r * sp[:, cs]
        a_ref[:, cs] = jnp.exp(log_a)
        th = jnp.tanh(log_a)
        b_ref[:, cs] = jnp.sqrt(-2.0 * th / (1.0 - th)) * (i * xc[:, cs])

    def body(t, h):
        tt = t + dr * (steps - 1 - 2 * t)
        rs = pl.ds(pl.multiple_of(tt * SUBLANES, SUBLANES), SUBLANES)
        h = a_ref[rs, :] * h + b_ref[rs, :]
        y_ref[0, 0, rs, :] = h
        return h

    h = lax.fori_loop(0, steps, body, hc_ref[...], unroll=4)
    hc_ref[...] = h
    fin_ref[0, 0] = h


def _lru_scan(x_t, conv_w, conv_b, wa_bd, wx_bd, ba, bx, lam, h0, steps_total):
    ng, rows_total, w = x_t.shape
    ntile, tw = wa_bd.shape[1], wa_bd.shape[2]
    steps = _tile(steps_total, SCAN_STEPS)
    assert steps % 2 == 0
    nt = steps_total // steps
    rows = steps * SUBLANES
    halo = 2 * SUBLANES
    cwid = conv_w.shape[0]

    def tb(dr, j):
        return j + dr * (nt - 1 - 2 * j)

    return pl.pallas_call(
        functools.partial(_lru_kernel, steps=steps, nt=nt, ntile=ntile, tw=tw),
        grid=(ng, 2, nt),
        in_specs=[pl.BlockSpec((1, rows, w), lambda g, dr, j: (g, tb(dr, j), 0)),
                  pl.BlockSpec((1, halo, w), lambda g, dr, j: (g, jnp.maximum(tb(dr, j) * (rows // halo) - 1, 0), 0)),
                  pl.BlockSpec((1, SUBLANES, w), lambda g, dr, j: (g, jnp.minimum((tb(dr, j) + 1) * steps, nt * steps - 1), 0)),
                  pl.BlockSpec((cwid, w), lambda g, dr, j: (0, 0)),
                  pl.BlockSpec((1, w), lambda g, dr, j: (0, 0)),
                  pl.BlockSpec((1, ntile, tw, tw), lambda g, dr, j: (dr, 0, 0, 0)),
                  pl.BlockSpec((1, ntile, tw, tw), lambda g, dr, j: (dr, 0, 0, 0)),
                  pl.BlockSpec((1, 1, w), lambda g, dr, j: (dr, 0, 0)),
                  pl.BlockSpec((1, 1, w), lambda g, dr, j: (dr, 0, 0)),
                  pl.BlockSpec((1, 1, w), lambda g, dr, j: (dr, 0, 0)),
                  pl.BlockSpec((1, 1, SUBLANES, w), lambda g, dr, j: (g, dr, 0, 0))],
        out_specs=[pl.BlockSpec((1, 1, rows, w), lambda g, dr, j: (dr, g, tb(dr, j), 0)),
                   pl.BlockSpec((1, 1, SUBLANES, w), lambda g, dr, j: (g, dr, 0, 0))],
        out_shape=[jax.ShapeDtypeStruct((2, ng, rows_total, w), F32),
                   jax.ShapeDtypeStruct((ng, 2, SUBLANES, w), F32)],
        scratch_shapes=[pltpu.VMEM((rows, w), F32), pltpu.VMEM((rows, w), F32), pltpu.VMEM((SUBLANES, w), F32)],
        compiler_params=_cparams(("arbitrary", "arbitrary", "arbitrary")),
    )(x_t, x_t, x_t, conv_w, conv_b.reshape(1, w), wa_bd, wx_bd, ba.reshape(2, 1, w), bx.reshape(2, 1, w),
      lam.reshape(2, 1, w), h0)


def _split3(x):
    x1 = x.astype(BF16)
    r1 = x - x1.astype(F32)
    x2 = r1.astype(BF16)
    x3 = (r1 - x2.astype(F32)).astype(BF16)
    return x1, x2, x3


def _tri_sums(tri_bf, parts):
    return sum(jnp.dot(tri_bf, p, preferred_element_type=F32) for p in parts)


def _log_sigmoid(x):
    return jnp.minimum(x, 0.0) - jnp.log1p(jnp.exp(-jnp.abs(x)))


def _mlstm_kernel(*refs, length, chunk, has_state, want_final, scale):
    q_ref, k_ref, v_ref, o_ref, g_ref, bg_ref, ng_ref = refs[:7]
    pos = 7
    if has_state:
        c0_ref, n0_ref, m0_ref = refs[pos:pos + 3]
        pos += 3
    y_ref = refs[pos]
    pos += 1
    if want_final:
        cf_ref, nf_ref, mf_ref = refs[pos:pos + 3]
        pos += 3
    hsum_ref, c_scr, n_scr = refs[pos:pos + 3]

    nc = length // chunk
    t_ = chunk
    row_i = lax.broadcasted_iota(jnp.int32, (t_, t_), 0)
    col_i = lax.broadcasted_iota(jnp.int32, (t_, t_), 1)
    lower = col_i <= row_i
    upper = col_i >= row_i
    lower_bf = lower.astype(BF16)
    upper_bf = upper.astype(BF16)

    gates = g_ref[...] + bg_ref[...]
    lane = lax.broadcasted_iota(jnp.int32, gates.shape, 1)
    gm = jnp.where((lane % 2) == 1, _log_sigmoid(gates), gates)

    def sweep(c, backward, state, need_update):
        c_st, n_st, m_st = state
        rs = slice(c * t_, (c + 1) * t_)
        gc = gm[rs]
        parts = _split3(gc)
        cols = _tri_sums(upper_bf if backward else lower_bf, parts)
        rows_ = cols.T
        gct = gc.T
        ci, cf = (2, 3) if backward else (0, 1)
        bc_col = cols[:, cf:cf + 1]
        bc_row = rows_[cf:cf + 1, :]
        i_col = gc[:, ci:ci + 1]
        i_row = gct[ci:ci + 1, :]
        b_last = bc_col[0:1] if backward else bc_col[t_ - 1:t_]
        mask = upper if backward else lower

        qs = (q_ref[rs, :] * scale)
        qb = qs.astype(BF16)
        kf = k_ref[rs, :]
        kb = kf.astype(BF16)
        vf = v_ref[rs, :]
        s = lax.dot_general(qb, kb, (((1,), (1,)), ((), ())), preferred_element_type=F32)
        dm = jnp.where(mask, bc_col - bc_row + i_row, NEG_INF)
        inter = bc_col + m_st
        m_t = jnp.maximum(inter, jnp.max(dm, axis=1, keepdims=True))
        sc = s * jnp.exp(dm - m_t)
        num = jnp.dot(sc.astype(BF16), vf.astype(BF16), preferred_element_type=F32)
        den = jnp.sum(sc, axis=1, keepdims=True)
        if c_st is not None:
            w_inter = jnp.exp(inter - m_t)
            num = num + w_inter * lax.dot_general(qb, c_st.astype(BF16), (((1,), (1,)), ((), ())),
                                                  preferred_element_type=F32)
            den = den + w_inter * jnp.sum(qs * n_st, axis=1, keepdims=True)
        h = num / jnp.maximum(jnp.abs(den), jnp.exp(-m_t))
        if not need_update:
            return h, None

        g_col = b_last - bc_col + i_col
        m_new = jnp.maximum(b_last + m_st, jnp.max(g_col, axis=0, keepdims=True))
        wg = jnp.exp(g_col - m_new)
        c_new = lax.dot_general((vf * wg).astype(BF16), kb, (((0,), (0,)), ((), ())), preferred_element_type=F32)
        n_new = jnp.sum(wg * kf, axis=0, keepdims=True)
        if c_st is not None:
            decay = jnp.exp(b_last + m_st - m_new)
            c_new = c_new + decay * c_st
            n_new = n_new + decay * n_st
        return h, (c_new, n_new, m_new)

    m_fin = []
    for d in range(2):
        backward = d == 1
        if has_state:
            state = (c0_ref[0, d, 0], n0_ref[0, d, 0], m0_ref[0, 0, :, d:d + 1])
        else:
            state = (None, None, jnp.zeros((1, 1), F32))
        order = range(nc - 1, -1, -1) if backward else range(nc)
        for idx, c in enumerate(order):
            if idx > 0:
                state = (c_scr[...], n_scr[...], state[2])
            h, state = sweep(c, backward, state, want_final or idx < nc - 1)
            rs = slice(c * t_, (c + 1) * t_)
            if d == 0:
                hsum_ref[rs, :] = h
            else:
                hsum_ref[rs, :] += h
            if idx < nc - 1:
                c_scr[...] = state[0]
                n_scr[...] = state[1]
        if want_final:
            cf_ref[0, d, 0] = state[0]
            nf_ref[0, d, 0] = state[1]
            m_fin.append(state[2])
    if want_final:
        lane_f = lax.broadcasted_iota(jnp.int32, (1, LANES), 1)
        mf_ref[0, 0] = jnp.where(lane_f == 0, m_fin[0], jnp.where(lane_f == 1, m_fin[1], 0.0))

    hs = hsum_ref[...]
    hn = hs * lax.rsqrt(jnp.mean(hs * hs, axis=1, keepdims=True) + EPS) * ng_ref[...]
    y_ref[...] = (jax.nn.sigmoid(o_ref[...]) * hn).astype(y_ref.dtype)


def _mlstm(z, zg, bg, norm_g, offs, row0, nseq, length, nh, dv, dqk, state, want_final):
    chunk = _tile(length, ML_CHUNK)
    has_state = state is not None
    rb0 = row0 // length
    qo, ko, vo, oo = (offs["q"] // dqk, offs["k"] // dqk, offs["v"] // dv, offs["o"] // dv)
    gw = zg.shape[1] // nh
    in_specs = [pl.BlockSpec((length, dqk), lambda n, h: (rb0 + n, qo + h)),
                pl.BlockSpec((length, dqk), lambda n, h: (rb0 + n, ko + h)),
                pl.BlockSpec((length, dv), lambda n, h: (rb0 + n, vo + h)),
                pl.BlockSpec((length, dv), lambda n, h: (rb0 + n, oo + h)),
                pl.BlockSpec((length, gw), lambda n, h: (rb0 + n, h)),
                pl.BlockSpec((1, gw), lambda n, h: (0, h)),
                pl.BlockSpec((1, dv), lambda n, h: (0, h))]
    args = [z, z, z, z, zg, bg, norm_g]
    if has_state:
        c0, n0, m0 = state
        in_specs += [pl.BlockSpec((1, 2, 1, dv, dqk), lambda n, h: (n, 0, h, 0, 0)),
                     pl.BlockSpec((1, 2, 1, 1, dqk), lambda n, h: (n, 0, h, 0, 0)),
                     pl.BlockSpec((1, 1, 1, LANES), lambda n, h: (n, h, 0, 0))]
        args += [c0, n0, m0]
    out_specs = [pl.BlockSpec((length, dv), lambda n, h: (n, h))]
    out_shape = [jax.ShapeDtypeStruct((nseq * length, nh * dv), BF16)]
    if want_final:
        out_specs += [pl.BlockSpec((1, 2, 1, dv, dqk), lambda n, h: (n, 0, h, 0, 0)),
                      pl.BlockSpec((1, 2, 1, 1, dqk), lambda n, h: (n, 0, h, 0, 0)),
                      pl.BlockSpec((1, 1, 1, LANES), lambda n, h: (n, h, 0, 0))]
        out_shape += [jax.ShapeDtypeStruct((nseq, 2, nh, dv, dqk), F32),
                      jax.ShapeDtypeStruct((nseq, 2, nh, 1, dqk), F32),
                      jax.ShapeDtypeStruct((nseq, nh, 1, LANES), F32)]
    return pl.pallas_call(
        functools.partial(_mlstm_kernel, length=length, chunk=chunk, has_state=has_state,
                          want_final=want_final, scale=float(dqk) ** -0.5),
        grid=(nseq, nh),
        in_specs=in_specs,
        out_specs=out_specs,
        out_shape=out_shape,
        scratch_shapes=[pltpu.VMEM((length, dv), F32), pltpu.VMEM((dv, dqk), F32), pltpu.VMEM((1, dqk), F32)],
        compiler_params=_cparams(("arbitrary", "arbitrary")),
    )(*args)


def _mix_kernel(s5_ref, ml_ref, lru_ref, gb_ref, o_ref, *, w_s5, w_ml):
    o_ref[:, :w_s5] = s5_ref[...].astype(o_ref.dtype)
    o_ref[:, w_s5:w_s5 + w_ml] = ml_ref[...].astype(o_ref.dtype)
    o_ref[:, w_s5 + w_ml:] = ((lru_ref[0] + lru_ref[1]) * _gelu_tanh(gb_ref[...])).astype(o_ref.dtype)


def _mix(y_s5, y_ml, y_lru, z, gb_off):
    m, w_s5 = y_s5.shape
    w_ml = y_ml.shape[1]
    w_lru = y_lru.shape[2]
    tm = _tile(m, 512)
    gbo = gb_off // w_lru
    return pl.pallas_call(
        functools.partial(_mix_kernel, w_s5=w_s5, w_ml=w_ml),
        grid=(m // tm,),
        in_specs=[pl.BlockSpec((tm, w_s5), lambda i: (i, 0)),
                  pl.BlockSpec((tm, w_ml), lambda i: (i, 0)),
                  pl.BlockSpec((2, tm, w_lru), lambda i: (0, i, 0)),
                  pl.BlockSpec((tm, w_lru), lambda i: (i, gbo))],
        out_specs=pl.BlockSpec((tm, w_s5 + w_ml + w_lru), lambda i: (i, 0)),
        out_shape=jax.ShapeDtypeStruct((m, w_s5 + w_ml + w_lru), BF16),
        compiler_params=_cparams(("arbitrary",)),
    )(y_s5, y_ml, y_lru, z)


def _s5_params(lam_re, lam_im, log_dt, b_re, b_im, c_re, c_im):
    _, g, p = lam_re.shape
    h = b_re.shape[-1]
    cbw = min(MXU_DIM, g * h)
    gpc = cbw // h
    ncb = g // gpc
    lam = lax.complex(lam_re, lam_im)
    lam_bar = jnp.exp(lam * jnp.exp(log_dt)[..., None])
    b_bar = ((lam_bar - 1.0) / lam)[..., None] * lax.complex(b_re, b_im)
    eye = jnp.eye(gpc, dtype=F32)

    def expand_b(t):
        t = t.reshape(2, ncb, gpc, p, h)
        return jnp.einsum("dcgph,gk->dcghkp", t, eye).reshape(2, ncb, gpc * h, gpc * p)

    def expand_c(t):
        t = t.reshape(2, ncb, gpc, h, p)
        return jnp.einsum("dcghp,gk->dckpgh", t, eye).reshape(2, ncb, gpc * p, gpc * h)

    bd = jnp.concatenate([expand_b(b_bar.real), expand_b(b_bar.imag)], axis=-1).astype(BF16)
    cd = jnp.concatenate([expand_c(c_re), -expand_c(c_im)], axis=2).astype(BF16)
    lr = jnp.broadcast_to(lam_bar.real.reshape(2, 1, g * p), (2, SUBLANES, g * p))
    li = jnp.broadcast_to(lam_bar.imag.reshape(2, 1, g * p), (2, SUBLANES, g * p))
    return bd, cd, lr, li, (ncb, gpc)


def _s5_state_to_lanes(st, ncb, gpc):
    b, _, g, p, _ = st.shape
    t = st.reshape(b, 2, ncb, gpc, p, 2).transpose(1, 0, 2, 5, 3, 4)
    return t.reshape(2, b, 2 * g * p)


def _s5_lanes_to_state(fin, ncb, gpc, p):
    ng = fin.shape[0]
    t = fin.reshape(ng, 2, SUBLANES, ncb, 2, gpc, p).transpose(0, 2, 1, 3, 5, 6, 4)
    return t.reshape(ng * SUBLANES, 2, ncb * gpc, p, 2)


def _block_diag_tiles(w):
    _, nb, bs, _ = w.shape
    tw = min(MXU_DIM, nb * bs)
    bpt = tw // bs
    ntile = nb // bpt
    eye = jnp.eye(bpt, dtype=F32)
    t = jnp.einsum("dtbio,bk->dtbiko", w.reshape(2, ntile, bpt, bs, bs), eye)
    return t.reshape(2, ntile, tw, tw).astype(BF16)


def _to_scan_rows(t, nseq, length):
    c = t.shape[-1]
    t = t.reshape(nseq, length, c)
    pad = (-nseq) % SUBLANES
    if pad:
        t = jnp.pad(t, ((0, pad), (0, 0), (0, 0)))
    ng = (nseq + pad) // SUBLANES
    return t.reshape(ng, SUBLANES, length, c).transpose(0, 2, 1, 3).reshape(ng, length * SUBLANES, c)


def _from_scan_rows(t, nseq, length):
    lead = t.shape[:-3]
    ng, _, c = t.shape[-3:]
    t = t.reshape(lead + (ng, length, SUBLANES, c))
    t = jnp.moveaxis(t, -2, -3).reshape(lead + (ng * SUBLANES, length, c))
    return t[..., :nseq, :, :].reshape(lead + (nseq * length, c))


def _raster_to_columns(t, nseq, length):
    c = t.shape[-1]
    rows = length // GRID_W
    return t.reshape(nseq, rows, GRID_W, c).transpose(0, 2, 1, 3).reshape(nseq * length, c)


def _columns_to_raster(t, nseq, length):
    lead = t.shape[:-2]
    c = t.shape[-1]
    rows = length // GRID_W
    t = t.reshape(lead + (nseq, GRID_W, rows, c))
    return jnp.swapaxes(t, -2, -3).reshape(lead + (nseq * length, c))


def kernel(x_prompt, x_sample, c, state_s5, state_mlstm_c, state_mlstm_n, state_mlstm_m, state_lru, c_ctx, w_ada, b_ada, norm_gains, w_in, w_out, s5_lam_re, s5_lam_im, s5_log_dt, s5_b_re, s5_b_im, s5_c_re, s5_c_im, s5_d, s5_w_glu, s5_b_glu, ml_b_gate, ml_norm, lru_conv_w, lru_conv_b, lru_wa, lru_ba, lru_wx, lru_bx, lru_lam, w_ff1, w_ff2):
    bc, lc, d = x_prompt.shape
    bl, ll, _ = x_sample.shape
    depth = w_ada.shape[0]
    _, _, s5_g, s5_p = s5_lam_re.shape
    s5_h = s5_b_re.shape[-1]
    s5_w = s5_g * s5_h
    nh, dv, dqk = state_mlstm_c.shape[3:]
    ml_w, qk_w = nh * dv, nh * dqk
    lru_w = lru_lam.shape[-1]
    m_ctx, m_lat = bc * lc, bl * ll
    assert 1 + bl <= SUBLANES and bc % SUBLANES == 0

    sizes = (s5_w, qk_w, qk_w, ml_w, ml_w, 4 * nh, lru_w, lru_w)
    starts = [0]
    for s in sizes:
        starts.append(starts[-1] + s)
    offs = {"u": 0, "q": s5_w, "k": s5_w + qk_w, "v": s5_w + 2 * qk_w, "o": s5_w + 2 * qk_w + ml_w,
            "xb": s5_w + 2 * qk_w + 2 * ml_w, "gb": s5_w + 2 * qk_w + 2 * ml_w + lru_w}
    assert m_ctx % ll == 0 and offs["gb"] % lru_w == 0

    cond = jnp.zeros((SUBLANES, d), F32).at[0].set(c_ctx).at[1:1 + bl].set(c)
    mod = _ada_mod(cond, w_ada, b_ada).reshape(depth, SUBLANES, 6, d)

    x = jnp.concatenate([x_prompt.reshape(m_ctx, d), x_sample.reshape(m_lat, d)], axis=0)
    h1 = _prenorm(x, mod[0], norm_gains[0, 0], m_ctx, ll, 0, 1)

    finals = {k: [] for k in ("s5", "mc", "mn", "mm", "lru")}
    for l in range(depth):
        w_in_l = w_in[l]
        w_main = jnp.concatenate([w_in_l[:, :starts[5]].astype(BF16), w_in_l[:, starts[6]:].astype(BF16)], axis=1)
        w_gate = w_in_l[:, starts[5]:starts[6]].reshape(d, 4, nh).transpose(0, 2, 1)
        w_gate = jnp.pad(w_gate, ((0, 0), (0, 0), (0, LANES - 4))).reshape(d, nh * LANES).astype(BF16)
        bg = jnp.zeros((nh, LANES), F32).at[:, :4].set(ml_b_gate[l].T).reshape(1, nh * LANES)

        z = _matmul(h1, w_main)
        zg = _matmul(h1, w_gate)

        y_ml_c, mc_f, mn_f, mm_f = _mlstm(z, zg, bg, ml_norm[l].reshape(1, ml_w), offs, 0, bc, lc,
                                          nh, dv, dqk, None, True)
        m0 = jnp.zeros((bl, nh, 1, LANES), F32).at[:, :, 0, :2].set(state_mlstm_m[:, l].transpose(0, 2, 1))
        lat_state = (state_mlstm_c[:, l], state_mlstm_n[:, l].reshape(bl, 2, nh, 1, dqk), m0)
        (y_ml_l,) = _mlstm(z, zg, bg, ml_norm[l].reshape(1, ml_w), offs, m_ctx, bl, ll,
                           nh, dv, dqk, lat_state, False)
        y_ml = jnp.concatenate([y_ml_c, y_ml_l], axis=0)
        finals["mc"].append(mc_f)
        finals["mn"].append(mn_f.reshape(bc, 2, nh, dqk))
        finals["mm"].append(mm_f[:, :, 0, :2].transpose(0, 2, 1))

        bd, cd, lr, li, (ncb, gpc) = _s5_params(s5_lam_re[l], s5_lam_im[l], s5_log_dt[l], s5_b_re[l], s5_b_im[l],
                                                s5_c_re[l], s5_c_im[l])
        u = z[:, offs["u"]:offs["u"] + s5_w]
        u_c = _to_scan_rows(u[:m_ctx], bc, lc)
        u_l = _to_scan_rows(u[m_ctx:], bl, ll)
        h0_c = jnp.zeros((u_c.shape[0], 2, SUBLANES, 2 * s5_g * s5_p), F32)
        h0_l = _s5_state_to_lanes(state_s5[:, l], ncb, gpc)
        h0_l = jnp.pad(h0_l, ((0, 0), (0, SUBLANES - bl), (0, 0)))[None]
        yd_c, fin_c = _s5_scan(u_c, bd, cd, lr, li, h0_c, lc)
        yd_l, _ = _s5_scan(u_l, bd, cd, lr, li, h0_l, ll)
        finals["s5"].append(_s5_lanes_to_state(fin_c, ncb, gpc, s5_p))
        d_row = s5_d[l].reshape(1, s5_w)
        wg_bf = s5_w_glu[l].astype(BF16)
        bgl = s5_b_glu[l].reshape(1, s5_w)
        ys_c = _s5_post(yd_c.reshape(2, -1, s5_w), u_c.reshape(-1, s5_w), d_row, wg_bf, bgl)
        ys_l = _s5_post(yd_l.reshape(2, -1, s5_w), u_l.reshape(-1, s5_w), d_row, wg_bf, bgl)
        y_s5 = jnp.concatenate([_from_scan_rows(ys_c.reshape(-1, lc * SUBLANES, s5_w), bc, lc),
                                _from_scan_rows(ys_l.reshape(-1, ll * SUBLANES, s5_w), bl, ll)], axis=0)

        xb = z[:, offs["xb"]:offs["xb"] + lru_w]
        x_c = _to_scan_rows(xb[:m_ctx], bc, lc)
        x_l = _to_scan_rows(_raster_to_columns(xb[m_ctx:], bl, ll), bl, ll)
        wa_bd = _block_diag_tiles(lru_wa[l])
        wx_bd = _block_diag_tiles(lru_wx[l])
        g0_c = jnp.zeros((x_c.shape[0], 2, SUBLANES, lru_w), F32)
        g0_l = jnp.pad(state_lru[:, l].transpose(1, 0, 2), ((0, 0), (0, SUBLANES - bl), (0, 0)))[None]
        lru_args = (lru_conv_w[l], lru_conv_b[l], wa_bd, wx_bd, lru_ba[l], lru_bx[l], lru_lam[l])
        hd_c, lfin_c = _lru_scan(x_c, *lru_args, g0_c, lc)
        hd_l, _ = _lru_scan(x_l, *lru_args, g0_l, ll)
        finals["lru"].append(lfin_c.transpose(0, 2, 1, 3).reshape(bc, 2, lru_w))
        y_lru = jnp.concatenate([_from_scan_rows(hd_c, bc, lc),
                                 _columns_to_raster(_from_scan_rows(hd_l, bl, ll), bl, ll)], axis=1)

        mixcat = _mix(y_s5, y_ml, y_lru, z, offs["gb"])
        mix = _matmul(mixcat, w_out[l].astype(BF16))
        x, h2 = _resnorm(x, mix, mod[l], mod[l], norm_gains[l, 1], norm_gains[l, 2], m_ctx, ll, 2, 3, 4, True)
        act = _matmul(h2, w_ff1[l].astype(BF16), out_dtype=BF16, relu2=True)
        ff = _matmul(act, w_ff2[l].astype(BF16))
        last = l == depth - 1
        nl = l if last else l + 1
        x, h1 = _resnorm(x, ff, mod[l], mod[nl], norm_gains[l, 3], norm_gains[nl, 0], m_ctx, ll, 5, 0, 1, not last)

    y_prompt = x[:m_ctx].reshape(bc, lc, d)
    y_sample = x[m_ctx:].reshape(bl, ll, d)
    return (y_prompt, y_sample, jnp.stack(finals["s5"], 1), jnp.stack(finals["mc"], 1), jnp.stack(finals["mn"], 1),
            jnp.stack(finals["mm"], 1), jnp.stack(finals["lru"], 1))
```

```python
import functools

import jax
import jax.numpy as jnp
from jax import lax
from jax.experimental import pallas as pl
from jax.experimental.pallas import tpu as pltpu

F32 = jnp.float32
BF16 = jnp.bfloat16

EPS = 1e-6
GRID_W = 64
LRU_C = 8.0
CONV_LEFT = 2
SUBLANES = 8
MXU_DIM = 256
LANES = 128
SCAN_STEPS = 32
S5_TAU = 16
ML_CHUNK = 256
VMEM_LIMIT = 56 * 1024 * 1024
NEG_INF = float("-inf")


def _cparams(sem):
    return pltpu.CompilerParams(dimension_semantics=sem, vmem_limit_bytes=VMEM_LIMIT)


def _sigmoid(x):
    return 0.5 * jnp.tanh(0.5 * x) + 0.5


def _tile(n, target):
    t = min(n, target)
    while n % t:
        t -= 1
    return t


def _ada_kernel(c_ref, w_ref, b_ref, o_ref):
    c = c_ref[...]
    s = (c * _sigmoid(c)).astype(BF16)
    o_ref[0] = jnp.dot(s, w_ref[0].astype(BF16), preferred_element_type=F32) + b_ref[0]


def _ada_mod(cond, w_ada, b_ada):
    depth, d, n = w_ada.shape
    tn = _tile(n, 1024)
    return pl.pallas_call(
        _ada_kernel,
        grid=(depth, n // tn),
        in_specs=[pl.BlockSpec((SUBLANES, d), lambda l, j: (0, 0)),
                  pl.BlockSpec((1, d, tn), lambda l, j: (l, 0, j)),
                  pl.BlockSpec((1, 1, tn), lambda l, j: (l, 0, j))],
        out_specs=pl.BlockSpec((1, SUBLANES, tn), lambda l, j: (l, 0, j)),
        out_shape=jax.ShapeDtypeStruct((depth, SUBLANES, n), F32),
        compiler_params=_cparams(("arbitrary", "arbitrary")),
    )(cond, w_ada, b_ada.reshape(depth, 1, n))


def _rms(x, g):
    return x * lax.rsqrt(jnp.mean(x * x, axis=-1, keepdims=True) + EPS) * g


def _prenorm_kernel(x_ref, mod_ref, g_ref, h_ref, *, shift_idx, scale_idx):
    m = mod_ref[0]
    hn = _rms(x_ref[...], g_ref[...])
    h_ref[...] = (hn * (1.0 + m[scale_idx:scale_idx + 1]) + m[shift_idx:shift_idx + 1]).astype(h_ref.dtype)


def _resnorm_kernel(x_ref, y_ref, mod_ref, modn_ref, gp_ref, gn_ref, xo_ref, *rest,
                    gate_idx, shift_idx, scale_idx):
    m = mod_ref[0]
    x = x_ref[...] + m[gate_idx:gate_idx + 1] * _rms(y_ref[...], gp_ref[...])
    xo_ref[...] = x
    if rest:
        mn = modn_ref[0]
        hn = _rms(x, gn_ref[...])
        rest[0][...] = (hn * (1.0 + mn[scale_idx:scale_idx + 1]) + mn[shift_idx:shift_idx + 1]).astype(BF16)


def _gcd(a, b):
    while b:
        a, b = b, a % b
    return a


def _seg_map(tm, m_ctx, l_lat):
    nct = m_ctx // tm
    per = l_lat // tm

    def seg(i):
        return jnp.where(i < nct, 0, 1 + (i - nct) // per)

    return seg


def _prenorm(x, mod_l, gain, m_ctx, l_lat, shift_idx, scale_idx):
    m, d = x.shape
    tm = _tile(_gcd(m_ctx, l_lat), 256)
    seg = _seg_map(tm, m_ctx, l_lat)
    return pl.pallas_call(
        functools.partial(_prenorm_kernel, shift_idx=shift_idx, scale_idx=scale_idx),
        grid=(m // tm,),
        in_specs=[pl.BlockSpec((tm, d), lambda i: (i, 0)),
                  pl.BlockSpec((1, 6, d), lambda i: (seg(i), 0, 0)),
                  pl.BlockSpec((1, d), lambda i: (0, 0))],
        out_specs=pl.BlockSpec((tm, d), lambda i: (i, 0)),
        out_shape=jax.ShapeDtypeStruct((m, d), BF16),
        compiler_params=_cparams(("arbitrary",)),
    )(x, mod_l, gain.reshape(1, d))


def _resnorm(x, y, mod_l, mod_next, g_post, g_next, m_ctx, l_lat, gate_idx, shift_idx, scale_idx, with_next):
    m, d = x.shape
    tm = _tile(_gcd(m_ctx, l_lat), 256)
    seg = _seg_map(tm, m_ctx, l_lat)
    row = pl.BlockSpec((tm, d), lambda i: (i, 0))
    modspec = pl.BlockSpec((1, 6, d), lambda i: (seg(i), 0, 0))
    vec = pl.BlockSpec((1, d), lambda i: (0, 0))
    out_shape = [jax.ShapeDtypeStruct((m, d), F32)]
    out_specs = [row]
    if with_next:
        out_shape.append(jax.ShapeDtypeStruct((m, d), BF16))
        out_specs.append(row)
    outs = pl.pallas_call(
        functools.partial(_resnorm_kernel, gate_idx=gate_idx, shift_idx=shift_idx, scale_idx=scale_idx),
        grid=(m // tm,),
        in_specs=[row, row, modspec, modspec, vec, vec],
        out_specs=out_specs,
        out_shape=out_shape,
        compiler_params=_cparams(("arbitrary",)),
    )(x, y, mod_l, mod_next, g_post.reshape(1, d), g_next.reshape(1, d))
    return outs if with_next else (outs[0], None)


def _mm_kernel(a_ref, w_ref, o_ref, *, relu2):
    acc = jnp.dot(a_ref[...], w_ref[...], preferred_element_type=F32)
    if relu2:
        acc = jnp.square(jnp.maximum(acc, 0.0))
    o_ref[...] = acc.astype(o_ref.dtype)


def _mm_acc_kernel(a_ref, w_ref, o_ref, acc_ref):
    k = pl.program_id(2)

    @pl.when(k == 0)
    def _():
        acc_ref[...] = jnp.zeros_like(acc_ref)

    acc_ref[...] += jnp.dot(a_ref[...], w_ref[...], preferred_element_type=F32)

    @pl.when(k == pl.num_programs(2) - 1)
    def _():
        o_ref[...] = acc_ref[...].astype(o_ref.dtype)


def _matmul(a, w, layer, out_dtype=F32, relu2=False):
    m, k = a.shape
    n = w.shape[2]
    tm = _tile(m, 1024)
    tn = _tile(n, 1024)
    tk = _tile(k, 4096)
    if tk == k:
        return pl.pallas_call(
            functools.partial(_mm_kernel, relu2=relu2),
            grid=(n // tn, m // tm),
            in_specs=[pl.BlockSpec((tm, k), lambda j, i: (i, 0)),
                      pl.BlockSpec((None, k, tn), lambda j, i: (layer, 0, j))],
            out_specs=pl.BlockSpec((tm, tn), lambda j, i: (i, j)),
            out_shape=jax.ShapeDtypeStruct((m, n), out_dtype),
            compiler_params=_cparams(("arbitrary", "arbitrary")),
        )(a, w)
    assert not relu2
    return pl.pallas_call(
        _mm_acc_kernel,
        grid=(n // tn, m // tm, k // tk),
        in_specs=[pl.BlockSpec((tm, tk), lambda j, i, kk: (i, kk)),
                  pl.BlockSpec((None, tk, tn), lambda j, i, kk: (layer, kk, j))],
        out_specs=pl.BlockSpec((tm, tn), lambda j, i, kk: (i, j)),
        out_shape=jax.ShapeDtypeStruct((m, n), out_dtype),
        scratch_shapes=[pltpu.VMEM((tm, tn), F32)],
        compiler_params=_cparams(("arbitrary", "arbitrary", "arbitrary")),
    )(a, w)


def _s5_kernel(u_ref, bd_ref, cd_ref, lr_ref, li_ref, h0_ref, y_ref, st_ref, fin_ref, bu_ref, hc_ref,
               *, steps, ncb, cbw, gp, chunk):
    dr = pl.program_id(1)
    j = pl.program_id(2)
    sw = 2 * gp

    @pl.when(j == 0)
    def _():
        hc_ref[...] = h0_ref[0, 0]

    u = u_ref[0].astype(BF16)
    for cb in range(ncb):
        bu_ref[:, cb * sw:(cb + 1) * sw] = jnp.dot(
            u[:, cb * cbw:(cb + 1) * cbw], bd_ref[0, cb], preferred_element_type=F32)

    for cb in range(ncb):
        for off in range(0, gp, chunk):
            re = pl.ds(cb * sw + off, chunk)
            im = pl.ds(cb * sw + gp + off, chunk)
            lam = pl.ds(cb * gp + off, chunk)
            lr = lr_ref[0, :, lam]
            li = li_ref[0, :, lam]

            def body(t, carry, re=re, im=im, lr=lr, li=li):
                hr, hi = carry
                tt = t + dr * (steps - 1 - 2 * t)
                rows = pl.ds(pl.multiple_of(tt * SUBLANES, SUBLANES), SUBLANES)
                nr = lr * hr - li * hi + bu_ref[rows, re]
                ni = lr * hi + li * hr + bu_ref[rows, im]
                bu_ref[rows, re] = nr
                bu_ref[rows, im] = ni
                return nr, ni

            hr, hi = lax.fori_loop(0, steps, body, (hc_ref[:, re], hc_ref[:, im]), unroll=4)
            hc_ref[:, re] = hr
            hc_ref[:, im] = hi

    for cb in range(ncb):
        y_ref[0, 0, :, cb * cbw:(cb + 1) * cbw] = jnp.dot(
            bu_ref[:, cb * sw:(cb + 1) * sw].astype(BF16), cd_ref[0, cb], preferred_element_type=F32)
    st_ref[0, 0] = bu_ref[...]
    fin_ref[0, 0] = hc_ref[...]


def _s5_scan(u_t, bd, cd, lam_re, lam_im, h0, steps_total):
    ng, rows_total, w = u_t.shape
    ncb, cbw, sw = bd.shape[1], bd.shape[2], bd.shape[3]
    gp = sw // 2
    sl = ncb * gp
    steps = _tile(steps_total, SCAN_STEPS)
    nt = steps_total // steps
    rows = steps * SUBLANES
    chunk = _tile(gp, 512)

    def tb(dr, j):
        return j + dr * (nt - 1 - 2 * j)

    return pl.pallas_call(
        functools.partial(_s5_kernel, steps=steps, ncb=ncb, cbw=cbw, gp=gp, chunk=chunk),
        grid=(ng, 2, nt),
        in_specs=[pl.BlockSpec((1, rows, w), lambda g, dr, j: (g, tb(dr, j), 0)),
                  pl.BlockSpec((1, ncb, cbw, sw), lambda g, dr, j: (dr, 0, 0, 0)),
                  pl.BlockSpec((1, ncb, sw, cbw), lambda g, dr, j: (dr, 0, 0, 0)),
                  pl.BlockSpec((1, SUBLANES, sl), lambda g, dr, j: (dr, 0, 0)),
                  pl.BlockSpec((1, SUBLANES, sl), lambda g, dr, j: (dr, 0, 0)),
                  pl.BlockSpec((1, 1, SUBLANES, 2 * sl), lambda g, dr, j: (g, dr, 0, 0))],
        out_specs=[pl.BlockSpec((1, 1, rows, w), lambda g, dr, j: (dr, g, tb(dr, j), 0)),
                   pl.BlockSpec((1, 1, rows, 2 * sl), lambda g, dr, j: (dr, g, tb(dr, j), 0)),
                   pl.BlockSpec((1, 1, SUBLANES, 2 * sl), lambda g, dr, j: (g, dr, 0, 0))],
        out_shape=[jax.ShapeDtypeStruct((2, ng, rows_total, w), F32),
                   jax.ShapeDtypeStruct((2, ng, rows_total, 2 * sl), F32),
                   jax.ShapeDtypeStruct((ng, 2, SUBLANES, 2 * sl), F32)],
        scratch_shapes=[pltpu.VMEM((rows, 2 * sl), F32), pltpu.VMEM((SUBLANES, 2 * sl), F32)],
        compiler_params=_cparams(("arbitrary", "arbitrary", "arbitrary")),
    )(u_t, bd, cd, lam_re, lam_im, h0)


def _gelu_tanh(x):
    return 0.5 * x * (1.0 + jnp.tanh(0.7978845608028654 * (x + 0.044715 * (x * x * x))))


def _s5_blocks_kernel(ugc_ref, ugl_ref, w1_ref, w2_ref, lt_ref, h0_ref, yc_ref, yl_ref, fin_ref, hinc_ref, hinl_ref,
                      *, kw, sw, ctx_set, lat_set):
    p = sw // 2
    lane = lax.broadcasted_iota(jnp.int32, (1, sw), 1)
    sign = jnp.where(lane < p, -1.0, 1.0)
    for ug_ref, y_ref, hin_ref, (nsb, nseq), is_ctx in ((ugc_ref, yc_ref, hinc_ref, ctx_set, True),
                                                        (ugl_ref, yl_ref, hinl_ref, lat_set, False)):
        a = jnp.dot(ug_ref[...], w1_ref[...], preferred_element_type=F32)
        for d in range(2):
            l1 = lt_ref[2 * d:2 * d + 1, :]
            l2 = lt_ref[2 * d + 1:2 * d + 2, :] * sign
            e0 = 2 * kw + d * sw
            hcur = jnp.zeros((nseq, sw), F32) if is_ctx else h0_ref[d]
            for sb in (range(nsb) if d == 0 else range(nsb - 1, -1, -1)):
                rs = slice(sb * nseq, (sb + 1) * nseq)
                hin_ref[rs, d * sw:(d + 1) * sw] = hcur
                hcur = l1 * hcur + l2 * pltpu.roll(hcur, p, axis=1) + a[rs, e0:e0 + sw]
            if is_ctx:
                fin_ref[d] = hcur
        y_ref[...] = a[:, :kw] + a[:, kw:2 * kw] + jnp.dot(hin_ref[...].astype(BF16), w2_ref[...],
                                                           preferred_element_type=F32)


def _s5_blocks(ug_c, ug_l, w1, w2, lt, h0_l, ctx_set, lat_set):
    g, rc, kw = ug_c.shape
    rl = ug_l.shape[1]
    sw = lt.shape[-1]
    nseq_c, nseq_l = ctx_set[1], lat_set[1]
    return pl.pallas_call(
        functools.partial(_s5_blocks_kernel, kw=kw, sw=sw, ctx_set=ctx_set, lat_set=lat_set),
        grid=(g,),
        in_specs=[pl.BlockSpec((None, rc, kw), lambda i: (i, 0, 0)),
                  pl.BlockSpec((None, rl, kw), lambda i: (i, 0, 0)),
                  pl.BlockSpec((None, kw, w1.shape[2]), lambda i: (i, 0, 0)),
                  pl.BlockSpec((None, 2 * sw, kw), lambda i: (i, 0, 0)),
                  pl.BlockSpec((None, 4, sw), lambda i: (i, 0, 0)),
                  pl.BlockSpec((None, 2, nseq_l, sw), lambda i: (i, 0, 0, 0))],
        out_specs=[pl.BlockSpec((None, rc, kw), lambda i: (i, 0, 0)),
                   pl.BlockSpec((None, rl, kw), lambda i: (i, 0, 0)),
                   pl.BlockSpec((None, 2, nseq_c, sw), lambda i: (i, 0, 0, 0))],
        out_shape=[jax.ShapeDtypeStruct((g, rc, kw), F32),
                   jax.ShapeDtypeStruct((g, rl, kw), F32),
                   jax.ShapeDtypeStruct((g, 2, nseq_c, sw), F32)],
        scratch_shapes=[pltpu.VMEM((rc, 2 * sw), F32), pltpu.VMEM((rl, 2 * sw), F32)],
        compiler_params=_cparams(("arbitrary",)),
    )(ug_c, ug_l, w1, w2, lt, h0_l)


def _softplus(x):
    return jnp.maximum(x, 0.0) + jnp.log1p(jnp.exp(-jnp.abs(x)))


def _lru_kernel(x_ref, xp_ref, xn_ref, cw_ref, cb_ref, wa_ref, wx_ref, ba_ref, bx_ref, lam_ref, h0_ref,
                y_ref, fin_ref, a_ref, b_ref, hc_ref, *, steps, nt, ntile, tw):
    dr = pl.program_id(1)
    j = pl.program_id(2)
    tblk = j + dr * (nt - 1 - 2 * j)
    rows = steps * SUBLANES

    @pl.when(j == 0)
    def _():
        hc_ref[...] = h0_ref[0, 0]

    prev = jnp.where(tblk > 0, xp_ref[0], 0.0)
    nxt = jnp.where(tblk < nt - 1, xn_ref[0], 0.0)
    ext = jnp.concatenate([prev, x_ref[0], nxt], axis=0)
    cw = cw_ref[...]
    xc = cb_ref[...] + sum(cw[q:q + 1] * ext[q * SUBLANES:q * SUBLANES + rows] for q in range(cw.shape[0]))
    xcb = xc.astype(BF16)
    sp = _softplus(-lam_ref[0])
    for t in range(ntile):
        cs = slice(t * tw, (t + 1) * tw)
        r = _sigmoid(jnp.dot(xcb[:, cs], wa_ref[0, t], preferred_element_type=F32) + ba_ref[0, :, cs])
        i = _sigmoid(jnp.dot(xcb[:, cs], wx_ref[0, t], preferred_element_type=F32) + bx_ref[0, :, cs])
        log_a = (-LRU_C) * r * sp[:, cs]
        a = jnp.exp(log_a)
        a_ref[:, cs] = a
        b_ref[:, cs] = jnp.sqrt((1.0 - a) * (1.0 + a)) * (i * xc[:, cs])

    def body(t, h):
        tt = t + dr * (steps - 1 - 2 * t)
        rs = pl.ds(pl.multiple_of(tt * SUBLANES, SUBLANES), SUBLANES)
        h = a_ref[rs, :] * h + b_ref[rs, :]
        y_ref[0, 0, rs, :] = h
        return h

    h = lax.fori_loop(0, steps, body, hc_ref[...], unroll=4)
    hc_ref[...] = h
    fin_ref[0, 0] = h


def _lru_scan(x_t, conv_w, conv_b, wa_bd, wx_bd, ba, bx, lam, h0, steps_total):
    ng, rows_total, w = x_t.shape
    ntile, tw = wa_bd.shape[1], wa_bd.shape[2]
    steps = _tile(steps_total, SCAN_STEPS)
    assert steps % 2 == 0
    nt = steps_total // steps
    rows = steps * SUBLANES
    halo = 2 * SUBLANES
    cwid = conv_w.shape[0]

    def tb(dr, j):
        return j + dr * (nt - 1 - 2 * j)

    return pl.pallas_call(
        functools.partial(_lru_kernel, steps=steps, nt=nt, ntile=ntile, tw=tw),
        grid=(ng, 2, nt),
        in_specs=[pl.BlockSpec((1, rows, w), lambda g, dr, j: (g, tb(dr, j), 0)),
                  pl.BlockSpec((1, halo, w), lambda g, dr, j: (g, jnp.maximum(tb(dr, j) * (rows // halo) - 1, 0), 0)),
                  pl.BlockSpec((1, SUBLANES, w), lambda g, dr, j: (g, jnp.minimum((tb(dr, j) + 1) * steps, nt * steps - 1), 0)),
                  pl.BlockSpec((cwid, w), lambda g, dr, j: (0, 0)),
                  pl.BlockSpec((1, w), lambda g, dr, j: (0, 0)),
                  pl.BlockSpec((1, ntile, tw, tw), lambda g, dr, j: (dr, 0, 0, 0)),
                  pl.BlockSpec((1, ntile, tw, tw), lambda g, dr, j: (dr, 0, 0, 0)),
                  pl.BlockSpec((1, 1, w), lambda g, dr, j: (dr, 0, 0)),
                  pl.BlockSpec((1, 1, w), lambda g, dr, j: (dr, 0, 0)),
                  pl.BlockSpec((1, 1, w), lambda g, dr, j: (dr, 0, 0)),
                  pl.BlockSpec((1, 1, SUBLANES, w), lambda g, dr, j: (g, dr, 0, 0))],
        out_specs=[pl.BlockSpec((1, 1, rows, w), lambda g, dr, j: (dr, g, tb(dr, j), 0)),
                   pl.BlockSpec((1, 1, SUBLANES, w), lambda g, dr, j: (g, dr, 0, 0))],
        out_shape=[jax.ShapeDtypeStruct((2, ng, rows_total, w), F32),
                   jax.ShapeDtypeStruct((ng, 2, SUBLANES, w), F32)],
        scratch_shapes=[pltpu.VMEM((rows, w), F32), pltpu.VMEM((rows, w), F32), pltpu.VMEM((SUBLANES, w), F32)],
        compiler_params=_cparams(("arbitrary", "arbitrary", "arbitrary")),
    )(x_t, x_t, x_t, conv_w, conv_b.reshape(1, w), wa_bd, wx_bd, ba.reshape(2, 1, w), bx.reshape(2, 1, w),
      lam.reshape(2, 1, w), h0)


def _split3(x):
    x1 = x.astype(BF16)
    r1 = x - x1.astype(F32)
    x2 = r1.astype(BF16)
    x3 = (r1 - x2.astype(F32)).astype(BF16)
    return x1, x2, x3


def _tri_sums(tri_bf, parts):
    return sum(jnp.dot(tri_bf, p, preferred_element_type=F32) for p in parts)


def _log_sigmoid(x):
    return jnp.minimum(x, 0.0) - jnp.log1p(jnp.exp(-jnp.abs(x)))


def _mlstm_kernel(*refs, length, chunk, has_state, want_final, aliased, scale):
    q_ref, k_ref, v_ref, o_ref, g_ref, bg_ref, ng_ref = refs[:7]
    pos = 7
    if has_state:
        c0_ref, n0_ref, m0_ref = refs[pos:pos + 3]
        pos += 3
    if aliased:
        pos += 1
    y_ref = refs[pos]
    pos += 1
    if want_final:
        cf_ref, nf_ref, mf_ref = refs[pos:pos + 3]
        pos += 3
    hsum_ref, c_scr, n_scr = refs[pos:pos + 3]

    nc = length // chunk
    t_ = chunk
    row_i = lax.broadcasted_iota(jnp.int32, (t_, t_), 0)
    col_i = lax.broadcasted_iota(jnp.int32, (t_, t_), 1)
    lower = col_i <= row_i
    upper = col_i >= row_i
    lower_bf = lower.astype(BF16)
    upper_bf = upper.astype(BF16)

    gates = g_ref[...] + bg_ref[...]
    lane = lax.broadcasted_iota(jnp.int32, gates.shape, 1)
    gm = jnp.where((lane % 2) == 1, _log_sigmoid(gates), gates)

    def sweep(c, backward, state, need_update):
        c_st, n_st, m_st = state
        rs = slice(c * t_, (c + 1) * t_)
        gc = gm[rs]
        parts = _split3(gc)
        cols = _tri_sums(upper_bf if backward else lower_bf, parts)
        rows_ = cols.T
        gct = gc.T
        ci, cf = (2, 3) if backward else (0, 1)
        bc_col = cols[:, cf:cf + 1]
        bc_row = rows_[cf:cf + 1, :]
        i_col = gc[:, ci:ci + 1]
        i_row = gct[ci:ci + 1, :]
        b_last = bc_col[0:1] if backward else bc_col[t_ - 1:t_]
        mask = upper if backward else lower

        qs = (q_ref[rs, :] * scale)
        qb = qs.astype(BF16)
        kf = k_ref[rs, :]
        kb = kf.astype(BF16)
        vf = v_ref[rs, :]
        s = lax.dot_general(qb, kb, (((1,), (1,)), ((), ())), preferred_element_type=F32)
        dm = jnp.where(mask, bc_col - bc_row + i_row, NEG_INF)
        inter = bc_col + m_st
        m_t = jnp.maximum(inter, jnp.max(dm, axis=1, keepdims=True))
        sc = s * jnp.exp(dm - m_t)
        num = jnp.dot(sc.astype(BF16), vf.astype(BF16), preferred_element_type=F32)
        den = jnp.sum(sc, axis=1, keepdims=True)
        if c_st is not None:
            w_inter = jnp.exp(inter - m_t)
            num = num + w_inter * lax.dot_general(qb, c_st.astype(BF16), (((1,), (1,)), ((), ())),
                                                  preferred_element_type=F32)
            den = den + w_inter * jnp.sum(qs * n_st, axis=1, keepdims=True)
        h = num / jnp.maximum(jnp.abs(den), jnp.exp(-m_t))
        if not need_update:
            return h, None

        g_col = b_last - bc_col + i_col
        m_new = jnp.maximum(b_last + m_st, jnp.max(g_col, axis=0, keepdims=True))
        wg = jnp.exp(g_col - m_new)
        c_new = lax.dot_general((vf * wg).astype(BF16), kb, (((0,), (0,)), ((), ())), preferred_element_type=F32)
        n_new = jnp.sum(wg * kf, axis=0, keepdims=True)
        if c_st is not None:
            decay = jnp.exp(b_last + m_st - m_new)
            c_new = c_new + decay * c_st
            n_new = n_new + decay * n_st
        return h, (c_new, n_new, m_new)

    m_fin = []
    for d in range(2):
        backward = d == 1
        if has_state:
            state = (c0_ref[0, d, 0], n0_ref[0, d, 0], m0_ref[0, 0, :, d:d + 1])
        else:
            state = (None, None, jnp.zeros((1, 1), F32))
        order = range(nc - 1, -1, -1) if backward else range(nc)
        for idx, c in enumerate(order):
            if idx > 0:
                state = (c_scr[...], n_scr[...], state[2])
            h, state = sweep(c, backward, state, want_final or idx < nc - 1)
            rs = slice(c * t_, (c + 1) * t_)
            if d == 0:
                hsum_ref[rs, :] = h
            else:
                hsum_ref[rs, :] += h
            if idx < nc - 1:
                c_scr[...] = state[0]
                n_scr[...] = state[1]
        if want_final:
            cf_ref[0, d, 0] = state[0]
            nf_ref[0, d, 0] = state[1]
            m_fin.append(state[2])
    if want_final:
        lane_f = lax.broadcasted_iota(jnp.int32, (1, LANES), 1)
        mf_ref[0, 0] = jnp.where(lane_f == 0, m_fin[0], jnp.where(lane_f == 1, m_fin[1], 0.0))

    hs = hsum_ref[...]
    hn = hs * lax.rsqrt(jnp.mean(hs * hs, axis=1, keepdims=True) + EPS) * ng_ref[...]
    y_ref[...] = (_sigmoid(o_ref[...]) * hn).astype(y_ref.dtype)


def _mlstm(z, zg, bg, norm_g, offs, row0, nseq, length, nh, dv, dqk, state, want_final,
           layer=0, depth=1, c_buf=None):
    chunk = _tile(length, ML_CHUNK)
    has_state = state is not None
    rb0 = row0 // length
    qo, ko, vo, oo = (offs["q"] // dqk, offs["k"] // dqk, offs["v"] // dv, offs["o"] // dv)
    gw = zg.shape[1] // nh
    in_specs = [pl.BlockSpec((length, dqk), lambda n, h: (rb0 + n, qo + h)),
                pl.BlockSpec((length, dqk), lambda n, h: (rb0 + n, ko + h)),
                pl.BlockSpec((length, dv), lambda n, h: (rb0 + n, vo + h)),
                pl.BlockSpec((length, dv), lambda n, h: (rb0 + n, oo + h)),
                pl.BlockSpec((length, gw), lambda n, h: (rb0 + n, h)),
                pl.BlockSpec((1, gw), lambda n, h: (0, h)),
                pl.BlockSpec((1, dv), lambda n, h: (0, h))]
    args = [z, z, z, z, zg, bg, norm_g]
    if has_state:
        c0, n0, m0 = state
        in_specs += [pl.BlockSpec((1, 2, 1, dv, dqk), lambda n, h: (n, 0, h, 0, 0)),
                     pl.BlockSpec((1, 2, 1, 1, dqk), lambda n, h: (n, 0, h, 0, 0)),
                     pl.BlockSpec((1, 1, 1, LANES), lambda n, h: (n, h, 0, 0))]
        args += [c0, n0, m0]
    aliases = {}
    if c_buf is not None:
        aliases = {len(args): 1}
        in_specs.append(pl.BlockSpec(memory_space=pl.ANY))
        args.append(c_buf)
    out_specs = [pl.BlockSpec((length, dv), lambda n, h: (n, h))]
    out_shape = [jax.ShapeDtypeStruct((nseq * length, nh * dv), BF16)]
    if want_final:
        out_specs += [pl.BlockSpec((1, None, 2, 1, dv, dqk), lambda n, h: (n, layer, 0, h, 0, 0)),
                      pl.BlockSpec((1, 2, 1, 1, dqk), lambda n, h: (n, 0, h, 0, 0)),
                      pl.BlockSpec((1, 1, 1, LANES), lambda n, h: (n, h, 0, 0))]
        out_shape += [jax.ShapeDtypeStruct((nseq, depth, 2, nh, dv, dqk), F32),
                      jax.ShapeDtypeStruct((nseq, 2, nh, 1, dqk), F32),
                      jax.ShapeDtypeStruct((nseq, nh, 1, LANES), F32)]
    return pl.pallas_call(
        functools.partial(_mlstm_kernel, length=length, chunk=chunk, has_state=has_state,
                          want_final=want_final, aliased=c_buf is not None, scale=float(dqk) ** -0.5),
        grid=(nseq, nh),
        in_specs=in_specs,
        out_specs=out_specs,
        out_shape=out_shape,
        input_output_aliases=aliases,
        scratch_shapes=[pltpu.VMEM((length, dv), F32), pltpu.VMEM((dv, dqk), F32), pltpu.VMEM((1, dqk), F32)],
        compiler_params=_cparams(("arbitrary", "arbitrary")),
    )(*args)


def _mix_kernel(s5c_ref, s5l_ref, u_ref, d_ref, wg_ref, bgl_ref, mlc_ref, mll_ref, lruc_ref, lrul_ref, gb_ref, o_ref,
                *, nct, w_s5, w_ml):
    is_ctx = pl.program_id(0) < nct
    y = _gelu_tanh(jnp.where(is_ctx, s5c_ref[...], s5l_ref[...]) + d_ref[...] * u_ref[...])
    gl = jnp.dot(y.astype(BF16), wg_ref[...], preferred_element_type=F32) + bgl_ref[...]
    o_ref[:, :w_s5] = (y * _sigmoid(gl)).astype(o_ref.dtype)
    o_ref[:, w_s5:w_s5 + w_ml] = jnp.where(is_ctx, mlc_ref[...], mll_ref[...]).astype(o_ref.dtype)
    lru = jnp.where(is_ctx, lruc_ref[0] + lruc_ref[1], lrul_ref[0] + lrul_ref[1])
    o_ref[:, w_s5 + w_ml:] = (lru * _gelu_tanh(gb_ref[...])).astype(o_ref.dtype)


def _mix(y_s5, y_ml, y_lru, z, u_off, gb_off, d_row, w_glu, b_glu, layer):
    m_ctx, w_s5 = y_s5[0].shape
    m_lat = y_s5[1].shape[0]
    w_ml = y_ml[0].shape[1]
    w_lru = y_lru[0].shape[2]
    tm = _tile(_gcd(m_ctx, m_lat), 256)
    nct = m_ctx // tm
    gbo = gb_off // w_lru

    def ctx(i):
        return jnp.minimum(i, nct - 1)

    def lat(i):
        return jnp.maximum(i - nct, 0)

    return pl.pallas_call(
        functools.partial(_mix_kernel, nct=nct, w_s5=w_s5, w_ml=w_ml),
        grid=((m_ctx + m_lat) // tm,),
        in_specs=[pl.BlockSpec((tm, w_s5), lambda i: (ctx(i), 0)),
                  pl.BlockSpec((tm, w_s5), lambda i: (lat(i), 0)),
                  pl.BlockSpec((tm, w_s5), lambda i: (i, u_off // w_s5)),
                  pl.BlockSpec((1, w_s5), lambda i: (0, 0)),
                  pl.BlockSpec((None, w_s5, w_s5), lambda i: (layer, 0, 0)),
                  pl.BlockSpec((1, w_s5), lambda i: (0, 0)),
                  pl.BlockSpec((tm, w_ml), lambda i: (ctx(i), 0)),
                  pl.BlockSpec((tm, w_ml), lambda i: (lat(i), 0)),
                  pl.BlockSpec((2, tm, w_lru), lambda i: (0, ctx(i), 0)),
                  pl.BlockSpec((2, tm, w_lru), lambda i: (0, lat(i), 0)),
                  pl.BlockSpec((tm, w_lru), lambda i: (i, gbo))],
        out_specs=pl.BlockSpec((tm, w_s5 + w_ml + w_lru), lambda i: (i, 0)),
        out_shape=jax.ShapeDtypeStruct((m_ctx + m_lat, w_s5 + w_ml + w_lru), BF16),
        compiler_params=_cparams(("arbitrary",)),
    )(y_s5[0], y_s5[1], z, d_row, w_glu, b_glu, y_ml[0], y_ml[1], y_lru[0], y_lru[1], z)


def _s5_params(lam_re, lam_im, log_dt, b_re, b_im, c_re, c_im):
    _, g, p = lam_re.shape
    h = b_re.shape[-1]
    cbw = min(MXU_DIM, g * h)
    gpc = cbw // h
    ncb = g // gpc
    dt = jnp.exp(log_dt)[..., None]
    mag = jnp.exp(lam_re * dt)
    lbr = mag * jnp.cos(lam_im * dt)
    lbi = mag * jnp.sin(lam_im * dt)
    den = lam_re * lam_re + lam_im * lam_im
    cr = (((lbr - 1.0) * lam_re + lbi * lam_im) / den)[..., None]
    ci = ((lbi * lam_re - (lbr - 1.0) * lam_im) / den)[..., None]
    bbr = cr * b_re - ci * b_im
    bbi = cr * b_im + ci * b_re
    eye = jnp.eye(gpc, dtype=F32)

    def expand_b(t):
        t = t.reshape(2, ncb, gpc, p, h)
        return jnp.einsum("dcgph,gk->dcghkp", t, eye).reshape(2, ncb, gpc * h, gpc * p)

    def expand_c(t):
        t = t.reshape(2, ncb, gpc, h, p)
        return jnp.einsum("dcghp,gk->dckpgh", t, eye).reshape(2, ncb, gpc * p, gpc * h)

    bd = jnp.concatenate([expand_b(bbr), expand_b(bbi)], axis=-1).astype(BF16)
    cd = jnp.concatenate([expand_c(c_re), -expand_c(c_im)], axis=2).astype(BF16)
    lr = jnp.broadcast_to(lbr.reshape(2, 1, g * p), (2, SUBLANES, g * p))
    li = jnp.broadcast_to(lbi.reshape(2, 1, g * p), (2, SUBLANES, g * p))
    return bd, cd, lr, li, (ncb, gpc)


def _s5_block_operands(bd, cd, lr, li, g, p, h, ncb, gpc):
    tau = S5_TAU
    w, sl = g * h, g * p
    nga = 2 * (h // SUBLANES)
    ngb = (2 * p) // SUBLANES
    half = nga // 2
    imp = jnp.tile(jnp.eye(h, dtype=F32), (1, g)).reshape(half, 1, SUBLANES, w)
    u = jnp.zeros((nga + ngb, tau, SUBLANES, w), F32)
    u = u.at[:half, 0:1].set(imp).at[half:nga, tau - 1:tau].set(imp)
    basis = jnp.broadcast_to(jnp.eye(2 * p, dtype=F32).reshape(2 * p, 1, 2, 1, p), (2 * p, ncb, 2, gpc, p))
    h0 = jnp.zeros((nga + ngb, 2, SUBLANES, 2 * sl), F32)
    h0 = h0.at[nga:].set(jnp.broadcast_to(basis.reshape(ngb, 1, SUBLANES, 2 * sl), (ngb, 2, SUBLANES, 2 * sl)))
    y, st, fin = _s5_scan(u.reshape(nga + ngb, tau * SUBLANES, w), bd, cd, lr, li, h0, tau)

    def out_resp(yd, g0):
        t = yd[g0:g0 + half].reshape(half, tau, SUBLANES, g, h)
        return t.transpose(1, 3, 0, 2, 4).reshape(tau, g, h, h)

    def state_resp(sd, g0):
        t = sd[g0:g0 + half].reshape(half, tau, SUBLANES, ncb, 2, gpc, p)
        return t.transpose(3, 5, 1, 0, 2, 4, 6).reshape(g, tau, h, 2 * p)

    r_i = jnp.arange(tau)[:, None]
    s_i = jnp.arange(tau)[None, :]

    def toeplitz(resp, lag):
        k = jnp.where((lag >= 0)[:, :, None, None, None], resp[jnp.clip(lag, 0, tau - 1)], 0.0)
        return k.transpose(2, 0, 3, 1, 4).reshape(g, tau * h, tau * h)

    k_f = toeplitz(out_resp(y[0], 0), s_i - r_i)
    k_b = toeplitz(out_resp(y[1], half)[::-1], r_i - s_i)
    e_f = state_resp(st[0], 0)[:, ::-1].reshape(g, tau * h, 2 * p)
    e_b = state_resp(st[1], half)[:, ::-1].reshape(g, tau * h, 2 * p)
    w1 = jnp.concatenate([k_f, k_b, e_f, e_b], axis=-1).astype(BF16)

    def carry_resp(yd):
        t = yd[nga:].reshape(ngb, tau, SUBLANES, g, h)
        return t.transpose(3, 0, 2, 1, 4).reshape(g, 2 * p, tau * h)

    w2 = jnp.concatenate([carry_resp(y[0]), carry_resp(y[1])], axis=1).astype(BF16)

    f = fin[nga:nga + p // SUBLANES].transpose(0, 2, 1, 3).reshape(p, 2, ncb, 2, gpc, p)
    lt = jnp.diagonal(f, axis1=0, axis2=5)
    l_re = lt[:, :, 0].reshape(2, g, p)
    l_im = lt[:, :, 1].reshape(2, g, p)
    rows = [jnp.concatenate([t[d], t[d]], axis=-1) for d in range(2) for t in (l_re, l_im)]
    return w1, w2, jnp.stack(rows, axis=1)


def _s5_group_rows(u, nseq, length, nseq_p, g, h):
    nsb = length // S5_TAU
    t = u.astype(BF16).reshape(nseq, nsb, S5_TAU, g, h)
    if nseq_p > nseq:
        t = jnp.pad(t, ((0, nseq_p - nseq), (0, 0), (0, 0), (0, 0), (0, 0)))
    return t.transpose(3, 1, 0, 2, 4).reshape(g, nsb * nseq_p, S5_TAU * h)


def _s5_natural_rows(y, nseq, length, nseq_p, g, h):
    nsb = length // S5_TAU
    t = y.reshape(g, nsb, nseq_p, S5_TAU, h).transpose(2, 1, 3, 0, 4)
    return t[:nseq].reshape(nseq * length, g * h)


def _block_diag_tiles(w):
    _, nb, bs, _ = w.shape
    tw = min(MXU_DIM, nb * bs)
    bpt = tw // bs
    ntile = nb // bpt
    eye = jnp.eye(bpt, dtype=F32)
    t = jnp.einsum("dtbio,bk->dtbiko", w.reshape(2, ntile, bpt, bs, bs), eye)
    return t.reshape(2, ntile, tw, tw).astype(BF16)


def _to_scan_rows(t, nseq, length):
    c = t.shape[-1]
    t = t.reshape(nseq, length, c)
    pad = (-nseq) % SUBLANES
    if pad:
        t = jnp.pad(t, ((0, pad), (0, 0), (0, 0)))
    ng = (nseq + pad) // SUBLANES
    return t.reshape(ng, SUBLANES, length, c).transpose(0, 2, 1, 3).reshape(ng, length * SUBLANES, c)


def _from_scan_rows(t, nseq, length):
    lead = t.shape[:-3]
    ng, _, c = t.shape[-3:]
    t = t.reshape(lead + (ng, length, SUBLANES, c))
    t = jnp.moveaxis(t, -2, -3).reshape(lead + (ng * SUBLANES, length, c))
    return t[..., :nseq, :, :].reshape(lead + (nseq * length, c))


def _raster_to_columns(t, nseq, length):
    c = t.shape[-1]
    rows = length // GRID_W
    return t.reshape(nseq, rows, GRID_W, c).transpose(0, 2, 1, 3).reshape(nseq * length, c)


def _columns_to_raster(t, nseq, length):
    lead = t.shape[:-2]
    c = t.shape[-1]
    rows = length // GRID_W
    t = t.reshape(lead + (nseq, GRID_W, rows, c))
    return jnp.swapaxes(t, -2, -3).reshape(lead + (nseq * length, c))


def kernel(x_prompt, x_sample, c, state_s5, state_mlstm_c, state_mlstm_n, state_mlstm_m, state_lru, c_ctx, w_ada, b_ada, norm_gains, w_in, w_out, s5_lam_re, s5_lam_im, s5_log_dt, s5_b_re, s5_b_im, s5_c_re, s5_c_im, s5_d, s5_w_glu, s5_b_glu, ml_b_gate, ml_norm, lru_conv_w, lru_conv_b, lru_wa, lru_ba, lru_wx, lru_bx, lru_lam, w_ff1, w_ff2):
    bc, lc, d = x_prompt.shape
    bl, ll, _ = x_sample.shape
    depth = w_ada.shape[0]
    _, _, s5_g, s5_p = s5_lam_re.shape
    s5_h = s5_b_re.shape[-1]
    s5_w = s5_g * s5_h
    nh, dv, dqk = state_mlstm_c.shape[3:]
    ml_w, qk_w = nh * dv, nh * dqk
    lru_w = lru_lam.shape[-1]
    m_ctx, m_lat = bc * lc, bl * ll
    assert 1 + bl <= SUBLANES and bc % SUBLANES == 0

    sizes = (s5_w, qk_w, qk_w, ml_w, ml_w, 4 * nh, lru_w, lru_w)
    starts = [0]
    for s in sizes:
        starts.append(starts[-1] + s)
    offs = {"u": 0, "q": s5_w, "k": s5_w + qk_w, "v": s5_w + 2 * qk_w, "o": s5_w + 2 * qk_w + ml_w,
            "xb": s5_w + 2 * qk_w + 2 * ml_w, "gb": s5_w + 2 * qk_w + 2 * ml_w + lru_w}
    assert m_ctx % ll == 0 and offs["gb"] % lru_w == 0

    cond = jnp.zeros((SUBLANES, d), F32).at[0].set(c_ctx).at[1:1 + bl].set(c)
    mod = _ada_mod(cond, w_ada, b_ada).reshape(depth, SUBLANES, 6, d)

    x = jnp.concatenate([x_prompt.reshape(m_ctx, d), x_sample.reshape(m_lat, d)], axis=0)
    h1 = _prenorm(x, mod[0], norm_gains[0, 0], m_ctx, ll, 0, 1)

    w_main = jnp.concatenate([w_in[:, :, :starts[5]], w_in[:, :, starts[6]:]], axis=2).astype(BF16)
    w_gate = w_in[:, :, starts[5]:starts[6]].reshape(depth, d, 4, nh).transpose(0, 1, 3, 2)
    w_gate = jnp.pad(w_gate, ((0, 0), (0, 0), (0, 0), (0, LANES - 4))).reshape(depth, d, nh * LANES).astype(BF16)
    w_out_bf = w_out.astype(BF16)
    w_ff1_bf = w_ff1.astype(BF16)
    w_ff2_bf = w_ff2.astype(BF16)
    w_glu_bf = s5_w_glu.astype(BF16)

    finals = {k: [] for k in ("s5", "mn", "mm", "lru")}
    mc_buf = None
    for l in range(depth):
        bg = jnp.zeros((nh, LANES), F32).at[:, :4].set(ml_b_gate[l].T).reshape(1, nh * LANES)

        z = _matmul(h1, w_main, l)
        zg = _matmul(h1, w_gate, l)

        y_ml_c, mc_buf, mn_f, mm_f = _mlstm(z, zg, bg, ml_norm[l].reshape(1, ml_w), offs, 0, bc, lc,
                                            nh, dv, dqk, None, True, layer=l, depth=depth, c_buf=mc_buf)
        m0 = jnp.zeros((bl, nh, 1, LANES), F32).at[:, :, 0, :2].set(state_mlstm_m[:, l].transpose(0, 2, 1))
        lat_state = (state_mlstm_c[:, l], state_mlstm_n[:, l].reshape(bl, 2, nh, 1, dqk), m0)
        (y_ml_l,) = _mlstm(z, zg, bg, ml_norm[l].reshape(1, ml_w), offs, m_ctx, bl, ll,
                           nh, dv, dqk, lat_state, False)
        finals["mn"].append(mn_f.reshape(bc, 2, nh, dqk))
        finals["mm"].append(mm_f[:, :, 0, :2].transpose(0, 2, 1))

        bd, cd, lr, li, (ncb, gpc) = _s5_params(s5_lam_re[l], s5_lam_im[l], s5_log_dt[l], s5_b_re[l], s5_b_im[l],
                                                s5_c_re[l], s5_c_im[l])
        w1, w2, lt = _s5_block_operands(bd, cd, lr, li, s5_g, s5_p, s5_h, ncb, gpc)
        u = z[:, offs["u"]:offs["u"] + s5_w]
        ug_c = _s5_group_rows(u[:m_ctx], bc, lc, bc, s5_g, s5_h)
        ug_l = _s5_group_rows(u[m_ctx:], bl, ll, SUBLANES, s5_g, s5_h)
        h0_l = state_s5[:, l].transpose(2, 1, 0, 4, 3).reshape(s5_g, 2, bl, 2 * s5_p)
        h0_l = jnp.pad(h0_l, ((0, 0), (0, 0), (0, SUBLANES - bl), (0, 0)))
        yg_c, yg_l, fin_c = _s5_blocks(ug_c, ug_l, w1, w2, lt, h0_l, (lc // S5_TAU, bc), (ll // S5_TAU, SUBLANES))
        finals["s5"].append(fin_c.reshape(s5_g, 2, bc, 2, s5_p).transpose(2, 1, 0, 4, 3))
        y_s5 = (_s5_natural_rows(yg_c, bc, lc, bc, s5_g, s5_h), _s5_natural_rows(yg_l, bl, ll, SUBLANES, s5_g, s5_h))

        xb = z[:, offs["xb"]:offs["xb"] + lru_w]
        x_c = _to_scan_rows(xb[:m_ctx], bc, lc)
        x_l = _to_scan_rows(_raster_to_columns(xb[m_ctx:], bl, ll), bl, ll)
        wa_bd = _block_diag_tiles(lru_wa[l])
        wx_bd = _block_diag_tiles(lru_wx[l])
        g0_c = jnp.zeros((x_c.shape[0], 2, SUBLANES, lru_w), F32)
        g0_l = jnp.pad(state_lru[:, l].transpose(1, 0, 2), ((0, 0), (0, SUBLANES - bl), (0, 0)))[None]
        lru_args = (lru_conv_w[l], lru_conv_b[l], wa_bd, wx_bd, lru_ba[l], lru_bx[l], lru_lam[l])
        hd_c, lfin_c = _lru_scan(x_c, *lru_args, g0_c, lc)
        hd_l, _ = _lru_scan(x_l, *lru_args, g0_l, ll)
        finals["lru"].append(lfin_c.transpose(0, 2, 1, 3).reshape(bc, 2, lru_w))
        y_lru = (_from_scan_rows(hd_c, bc, lc), _columns_to_raster(_from_scan_rows(hd_l, bl, ll), bl, ll))

        mixcat = _mix(y_s5, (y_ml_c, y_ml_l), y_lru, z, offs["u"], offs["gb"], s5_d[l].reshape(1, s5_w), w_glu_bf,
                      s5_b_glu[l].reshape(1, s5_w), l)
        mix = _matmul(mixcat, w_out_bf, l)
        x, h2 = _resnorm(x, mix, mod[l], mod[l], norm_gains[l, 1], norm_gains[l, 2], m_ctx, ll, 2, 3, 4, True)
        act = _matmul(h2, w_ff1_bf, l, out_dtype=BF16, relu2=True)
        ff = _matmul(act, w_ff2_bf, l)
        last = l == depth - 1
        nl = l if last else l + 1
        x, h1 = _resnorm(x, ff, mod[l], mod[nl], norm_gains[l, 3], norm_gains[nl, 0], m_ctx, ll, 5, 0, 1, not last)

    y_prompt = x[:m_ctx].reshape(bc, lc, d)
    y_sample = x[m_ctx:].reshape(bl, ll, d)
    return (y_prompt, y_sample, jnp.stack(finals["s5"], 1), mc_buf, jnp.stack(finals["mn"], 1),
            jnp.stack(finals["mm"], 1), jnp.stack(finals["lru"], 1))
```

```python
import functools

import jax
import jax.numpy as jnp
from jax import lax
from jax.experimental import pallas as pl
from jax.experimental.pallas import tpu as pltpu

F32 = jnp.float32
BF16 = jnp.bfloat16

EPS = 1e-6
GRID_W = 64
LRU_C = 8.0
CONV_LEFT = 2
SUBLANES = 8
MXU_DIM = 256
LANES = 128
SCAN_STEPS = 32
ML_CHUNK = 256
VMEM_LIMIT = 56 * 1024 * 1024
NEG_INF = float("-inf")


def _cparams(sem):
    return pltpu.CompilerParams(dimension_semantics=sem, vmem_limit_bytes=VMEM_LIMIT)


def _sigmoid(x):
    return 0.5 * jnp.tanh(0.5 * x) + 0.5


def _tile(n, target):
    t = min(n, target)
    while n % t:
        t -= 1
    return t


def _ada_kernel(c_ref, w_ref, b_ref, o_ref):
    c = c_ref[...]
    s = (c * _sigmoid(c)).astype(BF16)
    o_ref[0] = jnp.dot(s, w_ref[0].astype(BF16), preferred_element_type=F32) + b_ref[0]


def _ada_mod(cond, w_ada, b_ada):
    depth, d, n = w_ada.shape
    tn = _tile(n, 1024)
    return pl.pallas_call(
        _ada_kernel,
        grid=(depth, n // tn),
        in_specs=[pl.BlockSpec((SUBLANES, d), lambda l, j: (0, 0)),
                  pl.BlockSpec((1, d, tn), lambda l, j: (l, 0, j)),
                  pl.BlockSpec((1, 1, tn), lambda l, j: (l, 0, j))],
        out_specs=pl.BlockSpec((1, SUBLANES, tn), lambda l, j: (l, 0, j)),
        out_shape=jax.ShapeDtypeStruct((depth, SUBLANES, n), F32),
        compiler_params=_cparams(("arbitrary", "arbitrary")),
    )(cond, w_ada, b_ada.reshape(depth, 1, n))


def _rms(x, g):
    return x * lax.rsqrt(jnp.mean(x * x, axis=-1, keepdims=True) + EPS) * g


def _prenorm_kernel(x_ref, mod_ref, g_ref, h_ref, *, shift_idx, scale_idx):
    m = mod_ref[0]
    hn = _rms(x_ref[...], g_ref[...])
    h_ref[...] = (hn * (1.0 + m[scale_idx:scale_idx + 1]) + m[shift_idx:shift_idx + 1]).astype(h_ref.dtype)


def _resnorm_kernel(x_ref, y_ref, mod_ref, modn_ref, gp_ref, gn_ref, xo_ref, *rest,
                    gate_idx, shift_idx, scale_idx):
    m = mod_ref[0]
    x = x_ref[...] + m[gate_idx:gate_idx + 1] * _rms(y_ref[...].astype(F32), gp_ref[...])
    xo_ref[...] = x
    if rest:
        mn = modn_ref[0]
        hn = _rms(x, gn_ref[...])
        rest[0][...] = (hn * (1.0 + mn[scale_idx:scale_idx + 1]) + mn[shift_idx:shift_idx + 1]).astype(BF16)


def _gcd(a, b):
    while b:
        a, b = b, a % b
    return a


def _seg_map(tm, m_ctx, l_lat):
    nct = m_ctx // tm
    per = l_lat // tm

    def seg(i):
        return jnp.where(i < nct, 0, 1 + (i - nct) // per)

    return seg


def _prenorm(x, mod_l, gain, m_ctx, l_lat, shift_idx, scale_idx):
    m, d = x.shape
    tm = _tile(_gcd(m_ctx, l_lat), 256)
    seg = _seg_map(tm, m_ctx, l_lat)
    return pl.pallas_call(
        functools.partial(_prenorm_kernel, shift_idx=shift_idx, scale_idx=scale_idx),
        grid=(m // tm,),
        in_specs=[pl.BlockSpec((tm, d), lambda i: (i, 0)),
                  pl.BlockSpec((1, 6, d), lambda i: (seg(i), 0, 0)),
                  pl.BlockSpec((1, d), lambda i: (0, 0))],
        out_specs=pl.BlockSpec((tm, d), lambda i: (i, 0)),
        out_shape=jax.ShapeDtypeStruct((m, d), BF16),
        compiler_params=_cparams(("arbitrary",)),
    )(x, mod_l, gain.reshape(1, d))


def _resnorm(x, y, mod_l, mod_next, g_post, g_next, m_ctx, l_lat, gate_idx, shift_idx, scale_idx, with_next,
             row0=0, nrows=None):
    d = x.shape[1]
    m = x.shape[0] if nrows is None else nrows
    tm = _tile(_gcd(m_ctx, l_lat), 256)
    seg = _seg_map(tm, m_ctx, l_lat)
    rb0 = row0 // tm
    row_in = pl.BlockSpec((tm, d), lambda i: (i + rb0, 0))
    row = pl.BlockSpec((tm, d), lambda i: (i, 0))
    modspec = pl.BlockSpec((1, 6, d), lambda i: (seg(i + rb0), 0, 0))
    vec = pl.BlockSpec((1, d), lambda i: (0, 0))
    out_shape = [jax.ShapeDtypeStruct((m, d), F32)]
    out_specs = [row]
    if with_next:
        out_shape.append(jax.ShapeDtypeStruct((m, d), BF16))
        out_specs.append(row)
    outs = pl.pallas_call(
        functools.partial(_resnorm_kernel, gate_idx=gate_idx, shift_idx=shift_idx, scale_idx=scale_idx),
        grid=(m // tm,),
        in_specs=[row_in, row_in, modspec, modspec, vec, vec],
        out_specs=out_specs,
        out_shape=out_shape,
        compiler_params=_cparams(("arbitrary",)),
    )(x, y, mod_l, mod_next, g_post.reshape(1, d), g_next.reshape(1, d))
    return outs if with_next else (outs[0], None)


def _mm_kernel(a_ref, w_ref, o_ref, *, relu2):
    acc = jnp.dot(a_ref[...], w_ref[...], preferred_element_type=F32)
    if relu2:
        acc = jnp.square(jnp.maximum(acc, 0.0))
    o_ref[...] = acc.astype(o_ref.dtype)


def _mm_acc_kernel(a_ref, w_ref, o_ref, acc_ref):
    k = pl.program_id(2)

    @pl.when(k == 0)
    def _():
        acc_ref[...] = jnp.zeros_like(acc_ref)

    acc_ref[...] += jnp.dot(a_ref[...], w_ref[...], preferred_element_type=F32)

    @pl.when(k == pl.num_programs(2) - 1)
    def _():
        o_ref[...] = acc_ref[...].astype(o_ref.dtype)


def _matmul(a, w, layer, out_dtype=F32, relu2=False):
    m, k = a.shape
    n = w.shape[2]
    tm = _tile(m, 1024)
    tn = _tile(n, 1024)
    tk = _tile(k, 4096)
    if tk == k:
        return pl.pallas_call(
            functools.partial(_mm_kernel, relu2=relu2),
            grid=(n // tn, m // tm),
            in_specs=[pl.BlockSpec((tm, k), lambda j, i: (i, 0)),
                      pl.BlockSpec((None, k, tn), lambda j, i: (layer, 0, j))],
            out_specs=pl.BlockSpec((tm, tn), lambda j, i: (i, j)),
            out_shape=jax.ShapeDtypeStruct((m, n), out_dtype),
            compiler_params=_cparams(("arbitrary", "arbitrary")),
        )(a, w)
    assert not relu2
    return pl.pallas_call(
        _mm_acc_kernel,
        grid=(n // tn, m // tm, k // tk),
        in_specs=[pl.BlockSpec((tm, tk), lambda j, i, kk: (i, kk)),
                  pl.BlockSpec((None, tk, tn), lambda j, i, kk: (layer, kk, j))],
        out_specs=pl.BlockSpec((tm, tn), lambda j, i, kk: (i, j)),
        out_shape=jax.ShapeDtypeStruct((m, n), out_dtype),
        scratch_shapes=[pltpu.VMEM((tm, tn), F32)],
        compiler_params=_cparams(("arbitrary", "arbitrary", "arbitrary")),
    )(a, w)


def _s5_kernel(u_ref, bd_ref, cd_ref, lr_ref, li_ref, h0_ref, *rest, steps, ncb, cbw, gp, chunk):
    if len(rest) == 4:
        y_ref, fin_ref, bu_ref, hc_ref = rest
    else:
        y_ref = None
        fin_ref, bu_ref, hc_ref = rest
    dr = pl.program_id(1)
    j = pl.program_id(2)
    sw = 2 * gp

    @pl.when(j == 0)
    def _():
        hc_ref[...] = h0_ref[0, 0]

    u = u_ref[0].astype(BF16)
    for cb in range(ncb):
        bu_ref[:, cb * sw:(cb + 1) * sw] = jnp.dot(
            u[:, cb * cbw:(cb + 1) * cbw], bd_ref[0, cb], preferred_element_type=F32)

    for cb in range(ncb):
        for off in range(0, gp, chunk):
            re = pl.ds(cb * sw + off, chunk)
            im = pl.ds(cb * sw + gp + off, chunk)
            lam = pl.ds(cb * gp + off, chunk)
            lr = lr_ref[0, :, lam]
            li = li_ref[0, :, lam]

            def body(t, carry, re=re, im=im, lr=lr, li=li):
                hr, hi = carry
                tt = t + dr * (steps - 1 - 2 * t)
                rows = pl.ds(pl.multiple_of(tt * SUBLANES, SUBLANES), SUBLANES)
                nr = lr * hr - li * hi + bu_ref[rows, re]
                ni = lr * hi + li * hr + bu_ref[rows, im]
                bu_ref[rows, re] = nr
                bu_ref[rows, im] = ni
                return nr, ni

            hr, hi = lax.fori_loop(0, steps, body, (hc_ref[:, re], hc_ref[:, im]), unroll=4)
            hc_ref[:, re] = hr
            hc_ref[:, im] = hi

    if y_ref is not None:
        for cb in range(ncb):
            y_ref[0, 0, :, cb * cbw:(cb + 1) * cbw] = jnp.dot(
                bu_ref[:, cb * sw:(cb + 1) * sw].astype(BF16), cd_ref[0, cb], preferred_element_type=F32)
    fin_ref[0, 0] = hc_ref[...]


def _s5_scan(u_t, bd, cd, lam_re, lam_im, h0, steps_total, want_y=True):
    ng, rows_total, w = u_t.shape
    ncb, cbw, sw = bd.shape[1], bd.shape[2], bd.shape[3]
    gp = sw // 2
    sl = ncb * gp
    steps = _tile(steps_total, SCAN_STEPS)
    nt = steps_total // steps
    rows = steps * SUBLANES
    chunk = _tile(gp, 512)

    def tb(dr, j):
        return j + dr * (nt - 1 - 2 * j)

    out_specs = [pl.BlockSpec((1, 1, SUBLANES, 2 * sl), lambda g, dr, j: (g, dr, 0, 0))]
    out_shape = [jax.ShapeDtypeStruct((ng, 2, SUBLANES, 2 * sl), F32)]
    if want_y:
        out_specs.insert(0, pl.BlockSpec((1, 1, rows, w), lambda g, dr, j: (dr, g, tb(dr, j), 0)))
        out_shape.insert(0, jax.ShapeDtypeStruct((2, ng, rows_total, w), F32))
    outs = pl.pallas_call(
        functools.partial(_s5_kernel, steps=steps, ncb=ncb, cbw=cbw, gp=gp, chunk=chunk),
        grid=(ng, 2, nt),
        in_specs=[pl.BlockSpec((1, rows, w), lambda g, dr, j: (g, tb(dr, j), 0)),
                  pl.BlockSpec((1, ncb, cbw, sw), lambda g, dr, j: (dr, 0, 0, 0)),
                  pl.BlockSpec((1, ncb, sw, cbw), lambda g, dr, j: (dr, 0, 0, 0)),
                  pl.BlockSpec((1, SUBLANES, sl), lambda g, dr, j: (dr, 0, 0)),
                  pl.BlockSpec((1, SUBLANES, sl), lambda g, dr, j: (dr, 0, 0)),
                  pl.BlockSpec((1, 1, SUBLANES, 2 * sl), lambda g, dr, j: (g, dr, 0, 0))],
        out_specs=out_specs,
        out_shape=out_shape,
        scratch_shapes=[pltpu.VMEM((rows, 2 * sl), F32), pltpu.VMEM((SUBLANES, 2 * sl), F32)],
        compiler_params=_cparams(("arbitrary", "arbitrary", "arbitrary")),
    )(u_t, bd, cd, lam_re, lam_im, h0)
    return outs if want_y else (None, outs[0])


def _s5_seg_kernel(fin_ref, h0_ref, lr_ref, li_ref, o_ref, *, nb, nseg, seglen, ncb, gp):
    sw = 2 * gp
    for d in range(2):
        order = list(range(nseg)) if d == 0 else list(range(nseg - 1, -1, -1))
        for cb in range(ncb):
            re = slice(cb * sw, cb * sw + gp)
            im = slice(cb * sw + gp, (cb + 1) * sw)
            pr = lr_ref[d, 0:1, cb * gp:(cb + 1) * gp]
            pi = li_ref[d, 0:1, cb * gp:(cb + 1) * gp]
            n = 1
            while n < seglen:
                pr, pi = pr * pr - pi * pi, 2.0 * pr * pi
                n *= 2
            for b in range(nb):
                row = b * nseg + order[0]
                hr = h0_ref[d, row:row + 1, re]
                hi = h0_ref[d, row:row + 1, im]
                o_ref[d, row:row + 1, re] = hr
                o_ref[d, row:row + 1, im] = hi
                for idx in range(nseg - 1):
                    row = b * nseg + order[idx]
                    nr = fin_ref[d, row:row + 1, re]
                    ni = fin_ref[d, row:row + 1, im]
                    if idx > 0:
                        nr, ni = nr + pr * hr - pi * hi, ni + pr * hi + pi * hr
                    row = b * nseg + order[idx + 1]
                    o_ref[d, row:row + 1, re] = nr
                    o_ref[d, row:row + 1, im] = ni
                    hr, hi = nr, ni


def _s5_segment_starts(fin, h0, lam_re, lam_im, nb, nseg, seglen, ncb, gp):
    assert nb * nseg == SUBLANES and seglen & (seglen - 1) == 0
    return pl.pallas_call(
        functools.partial(_s5_seg_kernel, nb=nb, nseg=nseg, seglen=seglen, ncb=ncb, gp=gp),
        out_shape=jax.ShapeDtypeStruct(h0.shape, F32),
        compiler_params=pltpu.CompilerParams(vmem_limit_bytes=VMEM_LIMIT),
    )(fin, h0, lam_re, lam_im)


def _gelu_tanh(x):
    return 0.5 * x * (1.0 + jnp.tanh(0.7978845608028654 * (x + 0.044715 * (x * x * x))))


def _s5_post_kernel(yd_ref, u_ref, d_ref, w_ref, b_ref, o_ref):
    y = yd_ref[0] + yd_ref[1] + d_ref[...] * u_ref[...]
    y = _gelu_tanh(y)
    gl = jnp.dot(y.astype(BF16), w_ref[...], preferred_element_type=F32) + b_ref[...]
    o_ref[...] = (y * _sigmoid(gl)).astype(o_ref.dtype)


def _s5_post(yd, u_t, d_row, w_glu, b_glu):
    _, m, w = yd.shape
    tm = _tile(m, 512)
    return pl.pallas_call(
        _s5_post_kernel,
        grid=(m // tm,),
        in_specs=[pl.BlockSpec((2, tm, w), lambda i: (0, i, 0)),
                  pl.BlockSpec((tm, w), lambda i: (i, 0)),
                  pl.BlockSpec((1, w), lambda i: (0, 0)),
                  pl.BlockSpec((w, w), lambda i: (0, 0)),
                  pl.BlockSpec((1, w), lambda i: (0, 0))],
        out_specs=pl.BlockSpec((tm, w), lambda i: (i, 0)),
        out_shape=jax.ShapeDtypeStruct((m, w), BF16),
        compiler_params=_cparams(("arbitrary",)),
    )(yd, u_t, d_row, w_glu, b_glu)


def _softplus(x):
    return jnp.maximum(x, 0.0) + jnp.log1p(jnp.exp(-jnp.abs(x)))


def _lru_kernel(x_ref, xp_ref, xn_ref, cw_ref, cb_ref, wa_ref, wx_ref, ba_ref, bx_ref, lam_ref, h0_ref,
                y_ref, fin_ref, a_ref, b_ref, hc_ref, *, steps, nt, ntile, tw):
    dr = pl.program_id(1)
    j = pl.program_id(2)
    tblk = j + dr * (nt - 1 - 2 * j)
    rows = steps * SUBLANES

    @pl.when(j == 0)
    def _():
        hc_ref[...] = h0_ref[0, 0]

    prev = jnp.where(tblk > 0, xp_ref[0], 0.0)
    nxt = jnp.where(tblk < nt - 1, xn_ref[0], 0.0)
    ext = jnp.concatenate([prev, x_ref[0], nxt], axis=0)
    cw = cw_ref[...]
    xc = cb_ref[...] + sum(cw[q:q + 1] * ext[q * SUBLANES:q * SUBLANES + rows] for q in range(cw.shape[0]))
    xcb = xc.astype(BF16)
    sp = _softplus(-lam_ref[0])
    for t in range(ntile):
        cs = slice(t * tw, (t + 1) * tw)
        r = _sigmoid(jnp.dot(xcb[:, cs], wa_ref[0, t], preferred_element_type=F32) + ba_ref[0, :, cs])
        i = _sigmoid(jnp.dot(xcb[:, cs], wx_ref[0, t], preferred_element_type=F32) + bx_ref[0, :, cs])
        log_a = (-LRU_C) * r * sp[:, cs]
        a = jnp.exp(log_a)
        a_ref[:, cs] = a
        b_ref[:, cs] = jnp.sqrt((1.0 - a) * (1.0 + a)) * (i * xc[:, cs])

    def body(t, h):
        tt = t + dr * (steps - 1 - 2 * t)
        rs = pl.ds(pl.multiple_of(tt * SUBLANES, SUBLANES), SUBLANES)
        h = a_ref[rs, :] * h + b_ref[rs, :]
        y_ref[0, 0, rs, :] = h
        return h

    h = lax.fori_loop(0, steps, body, hc_ref[...], unroll=4)
    hc_ref[...] = h
    fin_ref[0, 0] = h


def _lru_scan(x_t, conv_w, conv_b, wa_bd, wx_bd, ba, bx, lam, h0, steps_total):
    ng, rows_total, w = x_t.shape
    ntile, tw = wa_bd.shape[1], wa_bd.shape[2]
    steps = _tile(steps_total, SCAN_STEPS)
    assert steps % 2 == 0
    nt = steps_total // steps
    rows = steps * SUBLANES
    halo = 2 * SUBLANES
    cwid = conv_w.shape[0]

    def tb(dr, j):
        return j + dr * (nt - 1 - 2 * j)

    return pl.pallas_call(
        functools.partial(_lru_kernel, steps=steps, nt=nt, ntile=ntile, tw=tw),
        grid=(ng, 2, nt),
        in_specs=[pl.BlockSpec((1, rows, w), lambda g, dr, j: (g, tb(dr, j), 0)),
                  pl.BlockSpec((1, halo, w), lambda g, dr, j: (g, jnp.maximum(tb(dr, j) * (rows // halo) - 1, 0), 0)),
                  pl.BlockSpec((1, SUBLANES, w), lambda g, dr, j: (g, jnp.minimum((tb(dr, j) + 1) * steps, nt * steps - 1), 0)),
                  pl.BlockSpec((cwid, w), lambda g, dr, j: (0, 0)),
                  pl.BlockSpec((1, w), lambda g, dr, j: (0, 0)),
                  pl.BlockSpec((1, ntile, tw, tw), lambda g, dr, j: (dr, 0, 0, 0)),
                  pl.BlockSpec((1, ntile, tw, tw), lambda g, dr, j: (dr, 0, 0, 0)),
                  pl.BlockSpec((1, 1, w), lambda g, dr, j: (dr, 0, 0)),
                  pl.BlockSpec((1, 1, w), lambda g, dr, j: (dr, 0, 0)),
                  pl.BlockSpec((1, 1, w), lambda g, dr, j: (dr, 0, 0)),
                  pl.BlockSpec((1, 1, SUBLANES, w), lambda g, dr, j: (g, dr, 0, 0))],
        out_specs=[pl.BlockSpec((1, 1, rows, w), lambda g, dr, j: (dr, g, tb(dr, j), 0)),
                   pl.BlockSpec((1, 1, SUBLANES, w), lambda g, dr, j: (g, dr, 0, 0))],
        out_shape=[jax.ShapeDtypeStruct((2, ng, rows_total, w), F32),
                   jax.ShapeDtypeStruct((ng, 2, SUBLANES, w), F32)],
        scratch_shapes=[pltpu.VMEM((rows, w), F32), pltpu.VMEM((rows, w), F32), pltpu.VMEM((SUBLANES, w), F32)],
        compiler_params=_cparams(("arbitrary", "arbitrary", "arbitrary")),
    )(x_t, x_t, x_t, conv_w, conv_b.reshape(1, w), wa_bd, wx_bd, ba.reshape(2, 1, w), bx.reshape(2, 1, w),
      lam.reshape(2, 1, w), h0)


def _split3(x):
    x1 = x.astype(BF16)
    r1 = x - x1.astype(F32)
    x2 = r1.astype(BF16)
    x3 = (r1 - x2.astype(F32)).astype(BF16)
    return x1, x2, x3


def _tri_sums(tri_bf, parts):
    return sum(jnp.dot(tri_bf, p, preferred_element_type=F32) for p in parts)


def _log_sigmoid(x):
    return jnp.minimum(x, 0.0) - jnp.log1p(jnp.exp(-jnp.abs(x)))


def _mlstm_kernel(*refs, length, chunk, has_state, want_final, aliased, scale):
    q_ref, k_ref, v_ref, o_ref, g_ref, bg_ref, ng_ref = refs[:7]
    pos = 7
    if has_state:
        c0_ref, n0_ref, m0_ref = refs[pos:pos + 3]
        pos += 3
    if aliased:
        pos += 1
    y_ref = refs[pos]
    pos += 1
    if want_final:
        cf_ref, nf_ref, mf_ref = refs[pos:pos + 3]
        pos += 3
    hsum_ref, c_scr, n_scr = refs[pos:pos + 3]

    nc = length // chunk
    t_ = chunk
    row_i = lax.broadcasted_iota(jnp.int32, (t_, t_), 0)
    col_i = lax.broadcasted_iota(jnp.int32, (t_, t_), 1)
    lower = col_i <= row_i
    upper = col_i >= row_i
    lower_bf = lower.astype(BF16)
    upper_bf = upper.astype(BF16)

    gates = g_ref[...] + bg_ref[...]
    lane = lax.broadcasted_iota(jnp.int32, gates.shape, 1)
    gm = jnp.where((lane % 2) == 1, _log_sigmoid(gates), gates)

    def sweep(c, backward, state, need_update):
        c_st, n_st, m_st = state
        rs = slice(c * t_, (c + 1) * t_)
        gc = gm[rs]
        parts = _split3(gc)
        cols = _tri_sums(upper_bf if backward else lower_bf, parts)
        rows_ = cols.T
        gct = gc.T
        ci, cf = (2, 3) if backward else (0, 1)
        bc_col = cols[:, cf:cf + 1]
        bc_row = rows_[cf:cf + 1, :]
        i_col = gc[:, ci:ci + 1]
        i_row = gct[ci:ci + 1, :]
        b_last = bc_col[0:1] if backward else bc_col[t_ - 1:t_]
        mask = upper if backward else lower

        qs = (q_ref[rs, :] * scale)
        qb = qs.astype(BF16)
        kf = k_ref[rs, :]
        kb = kf.astype(BF16)
        vf = v_ref[rs, :]
        s = lax.dot_general(qb, kb, (((1,), (1,)), ((), ())), preferred_element_type=F32)
        dm = jnp.where(mask, bc_col - bc_row + i_row, NEG_INF)
        inter = bc_col + m_st
        m_t = jnp.maximum(inter, jnp.max(dm, axis=1, keepdims=True))
        sc = s * jnp.exp(dm - m_t)
        num = jnp.dot(sc.astype(BF16), vf.astype(BF16), preferred_element_type=F32)
        den = jnp.sum(sc, axis=1, keepdims=True)
        if c_st is not None:
            w_inter = jnp.exp(inter - m_t)
            num = num + w_inter * lax.dot_general(qb, c_st.astype(BF16), (((1,), (1,)), ((), ())),
                                                  preferred_element_type=F32)
            den = den + w_inter * jnp.sum(qs * n_st, axis=1, keepdims=True)
        h = num / jnp.maximum(jnp.abs(den), jnp.exp(-m_t))
        if not need_update:
            return h, None

        g_col = b_last - bc_col + i_col
        m_new = jnp.maximum(b_last + m_st, jnp.max(g_col, axis=0, keepdims=True))
        wg = jnp.exp(g_col - m_new)
        c_new = lax.dot_general((vf * wg).astype(BF16), kb, (((0,), (0,)), ((), ())), preferred_element_type=F32)
        n_new = jnp.sum(wg * kf, axis=0, keepdims=True)
        if c_st is not None:
            decay = jnp.exp(b_last + m_st - m_new)
            c_new = c_new + decay * c_st
            n_new = n_new + decay * n_st
        return h, (c_new, n_new, m_new)

    m_fin = []
    for d in range(2):
        backward = d == 1
        if has_state:
            state = (c0_ref[0, d, 0], n0_ref[0, d, 0], m0_ref[0, 0, :, d:d + 1])
        else:
            state = (None, None, jnp.zeros((1, 1), F32))
        order = range(nc - 1, -1, -1) if backward else range(nc)
        for idx, c in enumerate(order):
            if idx > 0:
                state = (c_scr[...], n_scr[...], state[2])
            h, state = sweep(c, backward, state, want_final or idx < nc - 1)
            rs = slice(c * t_, (c + 1) * t_)
            if d == 0:
                hsum_ref[rs, :] = h
            else:
                hsum_ref[rs, :] += h
            if idx < nc - 1:
                c_scr[...] = state[0]
                n_scr[...] = state[1]
        if want_final:
            cf_ref[0, d, 0] = state[0]
            nf_ref[0, d, 0] = state[1]
            m_fin.append(state[2])
    if want_final:
        lane_f = lax.broadcasted_iota(jnp.int32, (1, LANES), 1)
        mf_ref[0, 0] = jnp.where(lane_f == 0, m_fin[0], jnp.where(lane_f == 1, m_fin[1], 0.0))

    hs = hsum_ref[...]
    hn = hs * lax.rsqrt(jnp.mean(hs * hs, axis=1, keepdims=True) + EPS) * ng_ref[...]
    y_ref[...] = (_sigmoid(o_ref[...]) * hn).astype(y_ref.dtype)


def _mlstm(z, zg, bg, norm_g, offs, row0, nseq, length, nh, dv, dqk, state, want_final,
           layer=0, depth=1, c_buf=None):
    chunk = _tile(length, ML_CHUNK)
    has_state = state is not None
    rb0 = row0 // length
    qo, ko, vo, oo = (offs["q"] // dqk, offs["k"] // dqk, offs["v"] // dv, offs["o"] // dv)
    gw = zg.shape[1] // nh
    in_specs = [pl.BlockSpec((length, dqk), lambda n, h: (rb0 + n, qo + h)),
                pl.BlockSpec((length, dqk), lambda n, h: (rb0 + n, ko + h)),
                pl.BlockSpec((length, dv), lambda n, h: (rb0 + n, vo + h)),
                pl.BlockSpec((length, dv), lambda n, h: (rb0 + n, oo + h)),
                pl.BlockSpec((length, gw), lambda n, h: (rb0 + n, h)),
                pl.BlockSpec((1, gw), lambda n, h: (0, h)),
                pl.BlockSpec((1, dv), lambda n, h: (0, h))]
    args = [z, z, z, z, zg, bg, norm_g]
    if has_state:
        c0, n0, m0 = state
        in_specs += [pl.BlockSpec((1, 2, 1, dv, dqk), lambda n, h: (n, 0, h, 0, 0)),
                     pl.BlockSpec((1, 2, 1, 1, dqk), lambda n, h: (n, 0, h, 0, 0)),
                     pl.BlockSpec((1, 1, 1, LANES), lambda n, h: (n, h, 0, 0))]
        args += [c0, n0, m0]
    aliases = {}
    if c_buf is not None:
        aliases = {len(args): 1}
        in_specs.append(pl.BlockSpec(memory_space=pl.ANY))
        args.append(c_buf)
    out_specs = [pl.BlockSpec((length, dv), lambda n, h: (n, h))]
    out_shape = [jax.ShapeDtypeStruct((nseq * length, nh * dv), BF16)]
    if want_final:
        out_specs += [pl.BlockSpec((1, None, 2, 1, dv, dqk), lambda n, h: (n, layer, 0, h, 0, 0)),
                      pl.BlockSpec((1, 2, 1, 1, dqk), lambda n, h: (n, 0, h, 0, 0)),
                      pl.BlockSpec((1, 1, 1, LANES), lambda n, h: (n, h, 0, 0))]
        out_shape += [jax.ShapeDtypeStruct((nseq, depth, 2, nh, dv, dqk), F32),
                      jax.ShapeDtypeStruct((nseq, 2, nh, 1, dqk), F32),
                      jax.ShapeDtypeStruct((nseq, nh, 1, LANES), F32)]
    return pl.pallas_call(
        functools.partial(_mlstm_kernel, length=length, chunk=chunk, has_state=has_state,
                          want_final=want_final, aliased=c_buf is not None, scale=float(dqk) ** -0.5),
        grid=(nseq, nh),
        in_specs=in_specs,
        out_specs=out_specs,
        out_shape=out_shape,
        input_output_aliases=aliases,
        scratch_shapes=[pltpu.VMEM((length, dv), F32), pltpu.VMEM((dv, dqk), F32), pltpu.VMEM((1, dqk), F32)],
        compiler_params=_cparams(("arbitrary", "arbitrary")),
    )(*args)


def _mix_kernel(s5c_ref, s5l_ref, mlc_ref, mll_ref, lruc_ref, lrul_ref, gb_ref, o_ref, *, nct, w_s5, w_ml):
    is_ctx = pl.program_id(0) < nct
    o_ref[:, :w_s5] = jnp.where(is_ctx, s5c_ref[...], s5l_ref[...]).astype(o_ref.dtype)
    o_ref[:, w_s5:w_s5 + w_ml] = jnp.where(is_ctx, mlc_ref[...], mll_ref[...]).astype(o_ref.dtype)
    lru = jnp.where(is_ctx, lruc_ref[0] + lruc_ref[1], lrul_ref[0] + lrul_ref[1])
    o_ref[:, w_s5 + w_ml:] = (lru * _gelu_tanh(gb_ref[...])).astype(o_ref.dtype)


def _mix(y_s5, y_ml, y_lru, zb, gb_off):
    m_ctx, w_s5 = y_s5[0].shape
    m_lat = y_s5[1].shape[0]
    w_ml = y_ml[0].shape[1]
    w_lru = y_lru[0].shape[2]
    tm = _tile(_gcd(m_ctx, m_lat), 512)
    nct = m_ctx // tm
    gbo = gb_off // w_lru

    def ctx(i):
        return jnp.minimum(i, nct - 1)

    def lat(i):
        return jnp.maximum(i - nct, 0)

    return pl.pallas_call(
        functools.partial(_mix_kernel, nct=nct, w_s5=w_s5, w_ml=w_ml),
        grid=((m_ctx + m_lat) // tm,),
        in_specs=[pl.BlockSpec((tm, w_s5), lambda i: (ctx(i), 0)),
                  pl.BlockSpec((tm, w_s5), lambda i: (lat(i), 0)),
                  pl.BlockSpec((tm, w_ml), lambda i: (ctx(i), 0)),
                  pl.BlockSpec((tm, w_ml), lambda i: (lat(i), 0)),
                  pl.BlockSpec((2, tm, w_lru), lambda i: (0, ctx(i), 0)),
                  pl.BlockSpec((2, tm, w_lru), lambda i: (0, lat(i), 0)),
                  pl.BlockSpec((tm, w_lru), lambda i: (i, gbo))],
        out_specs=pl.BlockSpec((tm, w_s5 + w_ml + w_lru), lambda i: (i, 0)),
        out_shape=jax.ShapeDtypeStruct((m_ctx + m_lat, w_s5 + w_ml + w_lru), BF16),
        compiler_params=_cparams(("arbitrary",)),
    )(y_s5[0], y_s5[1], y_ml[0], y_ml[1], y_lru[0], y_lru[1], zb)


def _s5_params(lam_re, lam_im, log_dt, b_re, b_im, c_re, c_im):
    _, g, p = lam_re.shape
    h = b_re.shape[-1]
    cbw = min(MXU_DIM, g * h)
    gpc = cbw // h
    ncb = g // gpc
    dt = jnp.exp(log_dt)[..., None]
    mag = jnp.exp(lam_re * dt)
    lbr = mag * jnp.cos(lam_im * dt)
    lbi = mag * jnp.sin(lam_im * dt)
    den = lam_re * lam_re + lam_im * lam_im
    cr = (((lbr - 1.0) * lam_re + lbi * lam_im) / den)[..., None]
    ci = ((lbi * lam_re - (lbr - 1.0) * lam_im) / den)[..., None]
    bbr = cr * b_re - ci * b_im
    bbi = cr * b_im + ci * b_re
    eye = jnp.eye(gpc, dtype=F32)

    def expand_b(t):
        t = t.reshape(2, ncb, gpc, p, h)
        return jnp.einsum("dcgph,gk->dcghkp", t, eye).reshape(2, ncb, gpc * h, gpc * p)

    def expand_c(t):
        t = t.reshape(2, ncb, gpc, h, p)
        return jnp.einsum("dcghp,gk->dckpgh", t, eye).reshape(2, ncb, gpc * p, gpc * h)

    bd = jnp.concatenate([expand_b(bbr), expand_b(bbi)], axis=-1).astype(BF16)
    cd = jnp.concatenate([expand_c(c_re), -expand_c(c_im)], axis=2).astype(BF16)
    lr = jnp.broadcast_to(lbr.reshape(2, 1, g * p), (2, SUBLANES, g * p))
    li = jnp.broadcast_to(lbi.reshape(2, 1, g * p), (2, SUBLANES, g * p))
    return bd, cd, lr, li, (ncb, gpc)


def _s5_state_to_lanes(st, ncb, gpc):
    b, _, g, p, _ = st.shape
    t = st.reshape(b, 2, ncb, gpc, p, 2).transpose(1, 0, 2, 5, 3, 4)
    return t.reshape(2, b, 2 * g * p)


def _s5_lanes_to_state(fin, ncb, gpc, p):
    ng = fin.shape[0]
    t = fin.reshape(ng, 2, SUBLANES, ncb, 2, gpc, p).transpose(0, 2, 1, 3, 5, 6, 4)
    return t.reshape(ng * SUBLANES, 2, ncb * gpc, p, 2)


def _block_diag_tiles(w):
    _, nb, bs, _ = w.shape
    tw = min(MXU_DIM, nb * bs)
    bpt = tw // bs
    ntile = nb // bpt
    eye = jnp.eye(bpt, dtype=F32)
    t = jnp.einsum("dtbio,bk->dtbiko", w.reshape(2, ntile, bpt, bs, bs), eye)
    return t.reshape(2, ntile, tw, tw).astype(BF16)


def _to_scan_rows(t, nseq, length):
    c = t.shape[-1]
    t = t.reshape(nseq, length, c)
    pad = (-nseq) % SUBLANES
    if pad:
        t = jnp.pad(t, ((0, pad), (0, 0), (0, 0)))
    ng = (nseq + pad) // SUBLANES
    return t.reshape(ng, SUBLANES, length, c).transpose(0, 2, 1, 3).reshape(ng, length * SUBLANES, c)


def _from_scan_rows(t, nseq, length):
    lead = t.shape[:-3]
    ng, _, c = t.shape[-3:]
    t = t.reshape(lead + (ng, length, SUBLANES, c))
    t = jnp.moveaxis(t, -2, -3).reshape(lead + (ng * SUBLANES, length, c))
    return t[..., :nseq, :, :].reshape(lead + (nseq * length, c))


def _raster_to_columns(t, nseq, length):
    c = t.shape[-1]
    rows = length // GRID_W
    return t.reshape(nseq, rows, GRID_W, c).transpose(0, 2, 1, 3).reshape(nseq * length, c)


def _columns_to_raster(t, nseq, length):
    lead = t.shape[:-2]
    c = t.shape[-1]
    rows = length // GRID_W
    t = t.reshape(lead + (nseq, GRID_W, rows, c))
    return jnp.swapaxes(t, -2, -3).reshape(lead + (nseq * length, c))


def kernel(x_prompt, x_sample, c, state_s5, state_mlstm_c, state_mlstm_n, state_mlstm_m, state_lru, c_ctx, w_ada, b_ada, norm_gains, w_in, w_out, s5_lam_re, s5_lam_im, s5_log_dt, s5_b_re, s5_b_im, s5_c_re, s5_c_im, s5_d, s5_w_glu, s5_b_glu, ml_b_gate, ml_norm, lru_conv_w, lru_conv_b, lru_wa, lru_ba, lru_wx, lru_bx, lru_lam, w_ff1, w_ff2):
    bc, lc, d = x_prompt.shape
    bl, ll, _ = x_sample.shape
    depth = w_ada.shape[0]
    _, _, s5_g, s5_p = s5_lam_re.shape
    s5_h = s5_b_re.shape[-1]
    s5_w = s5_g * s5_h
    nh, dv, dqk = state_mlstm_c.shape[3:]
    ml_w, qk_w = nh * dv, nh * dqk
    lru_w = lru_lam.shape[-1]
    m_ctx, m_lat = bc * lc, bl * ll
    assert 1 + bl <= SUBLANES and bc % SUBLANES == 0

    sizes = (s5_w, qk_w, qk_w, ml_w, ml_w, 4 * nh, lru_w, lru_w)
    starts = [0]
    for s in sizes:
        starts.append(starts[-1] + s)
    offs = {"u": 0, "q": s5_w, "k": s5_w + qk_w, "v": s5_w + 2 * qk_w, "o": s5_w + 2 * qk_w + ml_w,
            "xb": 0, "gb": lru_w}
    nseg = SUBLANES // bl
    seglen = ll // nseg
    assert m_ctx % ll == 0 and bl * nseg == SUBLANES and ll % nseg == 0

    cond = jnp.zeros((SUBLANES, d), F32).at[0].set(c_ctx).at[1:1 + bl].set(c)
    mod = _ada_mod(cond, w_ada, b_ada).reshape(depth, SUBLANES, 6, d)

    x = jnp.concatenate([x_prompt.reshape(m_ctx, d), x_sample.reshape(m_lat, d)], axis=0)
    h1 = _prenorm(x, mod[0], norm_gains[0, 0], m_ctx, ll, 0, 1)

    w_a = w_in[:, :, :starts[5]].astype(BF16)
    w_b = w_in[:, :, starts[6]:].astype(BF16)
    w_gate = w_in[:, :, starts[5]:starts[6]].reshape(depth, d, 4, nh).transpose(0, 1, 3, 2)
    w_gate = jnp.pad(w_gate, ((0, 0), (0, 0), (0, 0), (0, LANES - 4))).reshape(depth, d, nh * LANES).astype(BF16)
    w_out_bf = w_out.astype(BF16)
    w_ff1_bf = w_ff1.astype(BF16)
    w_ff2_bf = w_ff2.astype(BF16)
    w_glu_bf = s5_w_glu.astype(BF16)

    finals = {k: [] for k in ("s5", "mn", "mm", "lru")}
    mc_buf = None
    for l in range(depth):
        bg = jnp.zeros((nh, LANES), F32).at[:, :4].set(ml_b_gate[l].T).reshape(1, nh * LANES)

        z = _matmul(h1, w_a, l)
        zb = _matmul(h1, w_b, l)
        zg = _matmul(h1, w_gate, l)

        y_ml_c, mc_buf, mn_f, mm_f = _mlstm(z, zg, bg, ml_norm[l].reshape(1, ml_w), offs, 0, bc, lc,
                                            nh, dv, dqk, None, True, layer=l, depth=depth, c_buf=mc_buf)
        m0 = jnp.zeros((bl, nh, 1, LANES), F32).at[:, :, 0, :2].set(state_mlstm_m[:, l].transpose(0, 2, 1))
        lat_state = (state_mlstm_c[:, l], state_mlstm_n[:, l].reshape(bl, 2, nh, 1, dqk), m0)
        (y_ml_l,) = _mlstm(z, zg, bg, ml_norm[l].reshape(1, ml_w), offs, m_ctx, bl, ll,
                           nh, dv, dqk, lat_state, False)
        finals["mn"].append(mn_f.reshape(bc, 2, nh, dqk))
        finals["mm"].append(mm_f[:, :, 0, :2].transpose(0, 2, 1))

        bd, cd, lr, li, (ncb, gpc) = _s5_params(s5_lam_re[l], s5_lam_im[l], s5_log_dt[l], s5_b_re[l], s5_b_im[l],
                                                s5_c_re[l], s5_c_im[l])
        u = z[:, offs["u"]:offs["u"] + s5_w]
        u_c = _to_scan_rows(u[:m_ctx], bc, lc)
        u_l = _to_scan_rows(u[m_ctx:], bl * nseg, seglen)
        lanes = 2 * s5_g * s5_p
        h0_c = jnp.zeros((u_c.shape[0], 2, SUBLANES, lanes), F32)
        h0_t = _s5_state_to_lanes(state_s5[:, l], ncb, gpc)
        h0_l = jnp.zeros((2, bl, nseg, lanes), F32).at[0, :, 0].set(h0_t[0]).at[1, :, nseg - 1].set(h0_t[1])
        h0_l = h0_l.reshape(2, SUBLANES, lanes)
        yd_c, fin_c = _s5_scan(u_c, bd, cd, lr, li, h0_c, lc)
        _, fin_l = _s5_scan(u_l, bd, cd, lr, li, h0_l[None], seglen, want_y=False)
        h0_l = _s5_segment_starts(fin_l[0], h0_l, lr, li, bl, nseg, seglen, ncb, s5_g * s5_p // ncb)
        yd_l, _ = _s5_scan(u_l, bd, cd, lr, li, h0_l[None], seglen)
        finals["s5"].append(_s5_lanes_to_state(fin_c, ncb, gpc, s5_p))
        d_row = s5_d[l].reshape(1, s5_w)
        wg_bf = w_glu_bf[l]
        bgl = s5_b_glu[l].reshape(1, s5_w)
        ys_c = _s5_post(yd_c.reshape(2, -1, s5_w), u_c.reshape(-1, s5_w), d_row, wg_bf, bgl)
        ys_l = _s5_post(yd_l.reshape(2, -1, s5_w), u_l.reshape(-1, s5_w), d_row, wg_bf, bgl)
        y_s5 = (_from_scan_rows(ys_c.reshape(-1, lc * SUBLANES, s5_w), bc, lc),
                _from_scan_rows(ys_l.reshape(-1, seglen * SUBLANES, s5_w), bl * nseg, seglen))

        xb = zb[:, offs["xb"]:offs["xb"] + lru_w]
        x_c = _to_scan_rows(xb[:m_ctx], bc, lc)
        x_l = _to_scan_rows(_raster_to_columns(xb[m_ctx:], bl, ll), bl, ll)
        wa_bd = _block_diag_tiles(lru_wa[l])
        wx_bd = _block_diag_tiles(lru_wx[l])
        g0_c = jnp.zeros((x_c.shape[0], 2, SUBLANES, lru_w), F32)
        g0_l = jnp.pad(state_lru[:, l].transpose(1, 0, 2), ((0, 0), (0, SUBLANES - bl), (0, 0)))[None]
        lru_args = (lru_conv_w[l], lru_conv_b[l], wa_bd, wx_bd, lru_ba[l], lru_bx[l], lru_lam[l])
        hd_c, lfin_c = _lru_scan(x_c, *lru_args, g0_c, lc)
        hd_l, _ = _lru_scan(x_l, *lru_args, g0_l, ll)
        finals["lru"].append(lfin_c.transpose(0, 2, 1, 3).reshape(bc, 2, lru_w))
        y_lru = (_from_scan_rows(hd_c, bc, lc), _columns_to_raster(_from_scan_rows(hd_l, bl, ll), bl, ll))

        mixcat = _mix(y_s5, (y_ml_c, y_ml_l), y_lru, zb, offs["gb"])
        mix = _matmul(mixcat, w_out_bf, l, out_dtype=BF16)
        x, h2 = _resnorm(x, mix, mod[l], mod[l], norm_gains[l, 1], norm_gains[l, 2], m_ctx, ll, 2, 3, 4, True)
        act = _matmul(h2, w_ff1_bf, l, out_dtype=BF16, relu2=True)
        ff = _matmul(act, w_ff2_bf, l, out_dtype=BF16)
        if l < depth - 1:
            x, h1 = _resnorm(x, ff, mod[l], mod[l + 1], norm_gains[l, 3], norm_gains[l + 1, 0], m_ctx, ll, 5, 0, 1, True)
    last = (mod[depth - 1], mod[depth - 1], norm_gains[depth - 1, 3], norm_gains[depth - 1, 0], m_ctx, ll, 5, 0, 1, False)
    y_prompt = _resnorm(x, ff, *last, row0=0, nrows=m_ctx)[0].reshape(bc, lc, d)
    y_sample = _resnorm(x, ff, *last, row0=m_ctx, nrows=m_lat)[0].reshape(bl, ll, d)
    return (y_prompt, y_sample, jnp.stack(finals["s5"], 1), mc_buf, jnp.stack(finals["mn"], 1),
            jnp.stack(finals["mm"], 1), jnp.stack(finals["lru"], 1))
```

```python
import functools

import jax
import jax.numpy as jnp
from jax import lax
from jax.experimental import pallas as pl
from jax.experimental.pallas import tpu as pltpu

F32 = jnp.float32
BF16 = jnp.bfloat16

EPS = 1e-6
GRID_W = 64
LRU_C = 8.0
CONV_LEFT = 2
SUBLANES = 8
MXU_DIM = 256
LANES = 128
SCAN_STEPS = 32
ML_CHUNK = 256
VMEM_LIMIT = 56 * 1024 * 1024
NEG_INF = float("-inf")


def _cparams(sem):
    return pltpu.CompilerParams(dimension_semantics=sem, vmem_limit_bytes=VMEM_LIMIT)


def _sigmoid(x):
    return 0.5 * jnp.tanh(0.5 * x) + 0.5


def _tile(n, target):
    t = min(n, target)
    while n % t:
        t -= 1
    return t


def _ada_kernel(c_ref, w_ref, b_ref, o_ref):
    c = c_ref[...]
    s = (c * _sigmoid(c)).astype(BF16)
    o_ref[0] = jnp.dot(s, w_ref[0].astype(BF16), preferred_element_type=F32) + b_ref[0]


def _ada_mod(cond, w_ada, b_ada):
    depth, d, n = w_ada.shape
    tn = _tile(n, 1024)
    return pl.pallas_call(
        _ada_kernel,
        grid=(depth, n // tn),
        in_specs=[pl.BlockSpec((SUBLANES, d), lambda l, j: (0, 0)),
                  pl.BlockSpec((1, d, tn), lambda l, j: (l, 0, j)),
                  pl.BlockSpec((1, 1, tn), lambda l, j: (l, 0, j))],
        out_specs=pl.BlockSpec((1, SUBLANES, tn), lambda l, j: (l, 0, j)),
        out_shape=jax.ShapeDtypeStruct((depth, SUBLANES, n), F32),
        compiler_params=_cparams(("arbitrary", "arbitrary")),
    )(cond, w_ada, b_ada.reshape(depth, 1, n))


def _rms(x, g):
    return x * lax.rsqrt(jnp.mean(x * x, axis=-1, keepdims=True) + EPS) * g


def _prenorm_kernel(x_ref, mod_ref, g_ref, h_ref, *, shift_idx, scale_idx):
    m = mod_ref[0]
    hn = _rms(x_ref[...], g_ref[...])
    h_ref[...] = (hn * (1.0 + m[scale_idx:scale_idx + 1]) + m[shift_idx:shift_idx + 1]).astype(h_ref.dtype)


def _resnorm_kernel(x_ref, y_ref, mod_ref, modn_ref, gp_ref, gn_ref, xo_ref, *rest,
                    gate_idx, shift_idx, scale_idx):
    m = mod_ref[0]
    x = x_ref[...] + m[gate_idx:gate_idx + 1] * _rms(y_ref[...].astype(F32), gp_ref[...])
    xo_ref[...] = x
    if rest:
        mn = modn_ref[0]
        hn = _rms(x, gn_ref[...])
        rest[0][...] = (hn * (1.0 + mn[scale_idx:scale_idx + 1]) + mn[shift_idx:shift_idx + 1]).astype(BF16)


def _gcd(a, b):
    while b:
        a, b = b, a % b
    return a


def _seg_map(tm, m_ctx, l_lat):
    nct = m_ctx // tm
    per = l_lat // tm

    def seg(i):
        return jnp.where(i < nct, 0, 1 + (i - nct) // per)

    return seg


def _prenorm(x, mod_l, gain, m_ctx, l_lat, shift_idx, scale_idx):
    m, d = x.shape
    tm = _tile(_gcd(m_ctx, l_lat), 256)
    seg = _seg_map(tm, m_ctx, l_lat)
    return pl.pallas_call(
        functools.partial(_prenorm_kernel, shift_idx=shift_idx, scale_idx=scale_idx),
        grid=(m // tm,),
        in_specs=[pl.BlockSpec((tm, d), lambda i: (i, 0)),
                  pl.BlockSpec((1, 6, d), lambda i: (seg(i), 0, 0)),
                  pl.BlockSpec((1, d), lambda i: (0, 0))],
        out_specs=pl.BlockSpec((tm, d), lambda i: (i, 0)),
        out_shape=jax.ShapeDtypeStruct((m, d), BF16),
        compiler_params=_cparams(("arbitrary",)),
    )(x, mod_l, gain.reshape(1, d))


def _resnorm(x, y, mod_l, mod_next, g_post, g_next, m_ctx, l_lat, gate_idx, shift_idx, scale_idx, with_next,
             row0=0, nrows=None):
    d = x.shape[1]
    m = x.shape[0] if nrows is None else nrows
    tm = _tile(_gcd(m_ctx, l_lat), 256)
    seg = _seg_map(tm, m_ctx, l_lat)
    rb0 = row0 // tm
    row_in = pl.BlockSpec((tm, d), lambda i: (i + rb0, 0))
    row = pl.BlockSpec((tm, d), lambda i: (i, 0))
    modspec = pl.BlockSpec((1, 6, d), lambda i: (seg(i + rb0), 0, 0))
    vec = pl.BlockSpec((1, d), lambda i: (0, 0))
    out_shape = [jax.ShapeDtypeStruct((m, d), F32)]
    out_specs = [row]
    if with_next:
        out_shape.append(jax.ShapeDtypeStruct((m, d), BF16))
        out_specs.append(row)
    outs = pl.pallas_call(
        functools.partial(_resnorm_kernel, gate_idx=gate_idx, shift_idx=shift_idx, scale_idx=scale_idx),
        grid=(m // tm,),
        in_specs=[row_in, row_in, modspec, modspec, vec, vec],
        out_specs=out_specs,
        out_shape=out_shape,
        compiler_params=_cparams(("arbitrary",)),
    )(x, y, mod_l, mod_next, g_post.reshape(1, d), g_next.reshape(1, d))
    return outs if with_next else (outs[0], None)


def _mm_kernel(a_ref, w_ref, o_ref, *, relu2):
    acc = jnp.dot(a_ref[...], w_ref[...], preferred_element_type=F32)
    if relu2:
        acc = jnp.square(jnp.maximum(acc, 0.0))
    o_ref[...] = acc.astype(o_ref.dtype)


def _mm_acc_kernel(a_ref, w_ref, o_ref, acc_ref):
    k = pl.program_id(2)

    @pl.when(k == 0)
    def _():
        acc_ref[...] = jnp.zeros_like(acc_ref)

    acc_ref[...] += jnp.dot(a_ref[...], w_ref[...], preferred_element_type=F32)

    @pl.when(k == pl.num_programs(2) - 1)
    def _():
        o_ref[...] = acc_ref[...].astype(o_ref.dtype)


def _matmul(a, w, layer, out_dtype=F32, relu2=False, n=None):
    m, k = a.shape
    n = w.shape[2] if n is None else n
    tm = _tile(m, 1024)
    tn = _tile(n, 1024)
    tk = _tile(k, 4096)
    if tk == k:
        return pl.pallas_call(
            functools.partial(_mm_kernel, relu2=relu2),
            grid=(n // tn, m // tm),
            in_specs=[pl.BlockSpec((tm, k), lambda j, i: (i, 0)),
                      pl.BlockSpec((None, k, tn), lambda j, i: (layer, 0, j))],
            out_specs=pl.BlockSpec((tm, tn), lambda j, i: (i, j)),
            out_shape=jax.ShapeDtypeStruct((m, n), out_dtype),
            compiler_params=_cparams(("arbitrary", "arbitrary")),
        )(a, w)
    assert not relu2
    return pl.pallas_call(
        _mm_acc_kernel,
        grid=(n // tn, m // tm, k // tk),
        in_specs=[pl.BlockSpec((tm, tk), lambda j, i, kk: (i, kk)),
                  pl.BlockSpec((None, tk, tn), lambda j, i, kk: (layer, kk, j))],
        out_specs=pl.BlockSpec((tm, tn), lambda j, i, kk: (i, j)),
        out_shape=jax.ShapeDtypeStruct((m, n), out_dtype),
        scratch_shapes=[pltpu.VMEM((tm, tn), F32)],
        compiler_params=_cparams(("arbitrary", "arbitrary", "arbitrary")),
    )(a, w)


def _s5_kernel(u_ref, bd_ref, cd_ref, lr_ref, li_ref, h0_ref, *rest, steps, ncb, cbw, gp, chunk):
    if len(rest) == 4:
        y_ref, fin_ref, bu_ref, hc_ref = rest
    else:
        y_ref = None
        fin_ref, bu_ref, hc_ref = rest
    dr = pl.program_id(1)
    j = pl.program_id(2)
    sw = 2 * gp

    @pl.when(j == 0)
    def _():
        hc_ref[...] = h0_ref[0, 0]

    u = u_ref[0].astype(BF16)
    for cb in range(ncb):
        bu_ref[:, cb * sw:(cb + 1) * sw] = jnp.dot(
            u[:, cb * cbw:(cb + 1) * cbw], bd_ref[0, cb], preferred_element_type=F32)

    for cb in range(ncb):
        for off in range(0, gp, chunk):
            re = pl.ds(cb * sw + off, chunk)
            im = pl.ds(cb * sw + gp + off, chunk)
            lam = pl.ds(cb * gp + off, chunk)
            lr = lr_ref[0, :, lam]
            li = li_ref[0, :, lam]

            def body(t, carry, re=re, im=im, lr=lr, li=li):
                hr, hi = carry
                tt = t + dr * (steps - 1 - 2 * t)
                rows = pl.ds(pl.multiple_of(tt * SUBLANES, SUBLANES), SUBLANES)
                nr = lr * hr - li * hi + bu_ref[rows, re]
                ni = lr * hi + li * hr + bu_ref[rows, im]
                bu_ref[rows, re] = nr
                bu_ref[rows, im] = ni
                return nr, ni

            hr, hi = lax.fori_loop(0, steps, body, (hc_ref[:, re], hc_ref[:, im]), unroll=4)
            hc_ref[:, re] = hr
            hc_ref[:, im] = hi

    if y_ref is not None:
        for cb in range(ncb):
            y_ref[0, 0, :, cb * cbw:(cb + 1) * cbw] = jnp.dot(
                bu_ref[:, cb * sw:(cb + 1) * sw].astype(BF16), cd_ref[0, cb], preferred_element_type=F32)
    fin_ref[0, 0] = hc_ref[...]


def _s5_scan(u_t, bd, cd, lam_re, lam_im, h0, steps_total, want_y=True):
    ng, rows_total, w = u_t.shape
    ncb, cbw, sw = bd.shape[1], bd.shape[2], bd.shape[3]
    gp = sw // 2
    sl = ncb * gp
    steps = _tile(steps_total, SCAN_STEPS)
    nt = steps_total // steps
    rows = steps * SUBLANES
    chunk = _tile(gp, 512)

    def tb(dr, j):
        return j + dr * (nt - 1 - 2 * j)

    out_specs = [pl.BlockSpec((1, 1, SUBLANES, 2 * sl), lambda g, dr, j: (g, dr, 0, 0))]
    out_shape = [jax.ShapeDtypeStruct((ng, 2, SUBLANES, 2 * sl), F32)]
    if want_y:
        out_specs.insert(0, pl.BlockSpec((1, 1, rows, w), lambda g, dr, j: (dr, g, tb(dr, j), 0)))
        out_shape.insert(0, jax.ShapeDtypeStruct((2, ng, rows_total, w), F32))
    outs = pl.pallas_call(
        functools.partial(_s5_kernel, steps=steps, ncb=ncb, cbw=cbw, gp=gp, chunk=chunk),
        grid=(ng, 2, nt),
        in_specs=[pl.BlockSpec((1, rows, w), lambda g, dr, j: (g, tb(dr, j), 0)),
                  pl.BlockSpec((1, ncb, cbw, sw), lambda g, dr, j: (dr, 0, 0, 0)),
                  pl.BlockSpec((1, ncb, sw, cbw), lambda g, dr, j: (dr, 0, 0, 0)),
                  pl.BlockSpec((1, SUBLANES, sl), lambda g, dr, j: (dr, 0, 0)),
                  pl.BlockSpec((1, SUBLANES, sl), lambda g, dr, j: (dr, 0, 0)),
                  pl.BlockSpec((1, 1, SUBLANES, 2 * sl), lambda g, dr, j: (g, dr, 0, 0))],
        out_specs=out_specs,
        out_shape=out_shape,
        scratch_shapes=[pltpu.VMEM((rows, 2 * sl), F32), pltpu.VMEM((SUBLANES, 2 * sl), F32)],
        compiler_params=_cparams(("arbitrary", "arbitrary", "arbitrary")),
    )(u_t, bd, cd, lam_re, lam_im, h0)
    return outs if want_y else (None, outs[0])


def _s5_seg_kernel(fin_ref, h0_ref, lr_ref, li_ref, o_ref, *, nb, nseg, seglen, ncb, gp):
    sw = 2 * gp
    for d in range(2):
        order = list(range(nseg)) if d == 0 else list(range(nseg - 1, -1, -1))
        for cb in range(ncb):
            re = slice(cb * sw, cb * sw + gp)
            im = slice(cb * sw + gp, (cb + 1) * sw)
            pr = lr_ref[d, 0:1, cb * gp:(cb + 1) * gp]
            pi = li_ref[d, 0:1, cb * gp:(cb + 1) * gp]
            n = 1
            while n < seglen:
                pr, pi = pr * pr - pi * pi, 2.0 * pr * pi
                n *= 2
            for b in range(nb):
                row = b * nseg + order[0]
                hr = h0_ref[d, row:row + 1, re]
                hi = h0_ref[d, row:row + 1, im]
                o_ref[d, row:row + 1, re] = hr
                o_ref[d, row:row + 1, im] = hi
                for idx in range(nseg - 1):
                    row = b * nseg + order[idx]
                    nr = fin_ref[d, row:row + 1, re]
                    ni = fin_ref[d, row:row + 1, im]
                    if idx > 0:
                        nr, ni = nr + pr * hr - pi * hi, ni + pr * hi + pi * hr
                    row = b * nseg + order[idx + 1]
                    o_ref[d, row:row + 1, re] = nr
                    o_ref[d, row:row + 1, im] = ni
                    hr, hi = nr, ni


def _s5_segment_starts(fin, h0, lam_re, lam_im, nb, nseg, seglen, ncb, gp):
    assert nb * nseg == SUBLANES and seglen & (seglen - 1) == 0
    return pl.pallas_call(
        functools.partial(_s5_seg_kernel, nb=nb, nseg=nseg, seglen=seglen, ncb=ncb, gp=gp),
        out_shape=jax.ShapeDtypeStruct(h0.shape, F32),
        compiler_params=pltpu.CompilerParams(vmem_limit_bytes=VMEM_LIMIT),
    )(fin, h0, lam_re, lam_im)


def _gelu_tanh(x):
    return 0.5 * x * (1.0 + jnp.tanh(0.7978845608028654 * (x + 0.044715 * (x * x * x))))


def _s5_post_kernel(yd_ref, u_ref, d_ref, w_ref, b_ref, o_ref):
    y = yd_ref[0] + yd_ref[1] + d_ref[...] * u_ref[...]
    y = _gelu_tanh(y)
    gl = jnp.dot(y.astype(BF16), w_ref[...], preferred_element_type=F32) + b_ref[...]
    o_ref[...] = (y * _sigmoid(gl)).astype(o_ref.dtype)


def _s5_post(yd, u_t, d_row, w_glu, b_glu):
    _, m, w = yd.shape
    tm = _tile(m, 512)
    return pl.pallas_call(
        _s5_post_kernel,
        grid=(m // tm,),
        in_specs=[pl.BlockSpec((2, tm, w), lambda i: (0, i, 0)),
                  pl.BlockSpec((tm, w), lambda i: (i, 0)),
                  pl.BlockSpec((1, w), lambda i: (0, 0)),
                  pl.BlockSpec((w, w), lambda i: (0, 0)),
                  pl.BlockSpec((1, w), lambda i: (0, 0))],
        out_specs=pl.BlockSpec((tm, w), lambda i: (i, 0)),
        out_shape=jax.ShapeDtypeStruct((m, w), BF16),
        compiler_params=_cparams(("arbitrary",)),
    )(yd, u_t, d_row, w_glu, b_glu)


def _softplus(x):
    return jnp.maximum(x, 0.0) + jnp.log1p(jnp.exp(-jnp.abs(x)))


def _lru_kernel(x_ref, xp_ref, xn_ref, cw_ref, cb_ref, wa_ref, wx_ref, ba_ref, bx_ref, lam_ref, h0_ref,
                y_ref, fin_ref, a_ref, b_ref, hc_ref, *, steps, nt, ntile, tw):
    dr = pl.program_id(1)
    j = pl.program_id(2)
    tblk = j + dr * (nt - 1 - 2 * j)
    rows = steps * SUBLANES

    @pl.when(j == 0)
    def _():
        hc_ref[...] = h0_ref[0, 0]

    prev = jnp.where(tblk > 0, xp_ref[0], 0.0)
    nxt = jnp.where(tblk < nt - 1, xn_ref[0], 0.0)
    ext = jnp.concatenate([prev, x_ref[0], nxt], axis=0)
    cw = cw_ref[...]
    xc = cb_ref[...] + sum(cw[q:q + 1] * ext[q * SUBLANES:q * SUBLANES + rows] for q in range(cw.shape[0]))
    xcb = xc.astype(BF16)
    sp = _softplus(-lam_ref[0])
    for t in range(ntile):
        cs = slice(t * tw, (t + 1) * tw)
        r = _sigmoid(jnp.dot(xcb[:, cs], wa_ref[0, t], preferred_element_type=F32) + ba_ref[0, :, cs])
        i = _sigmoid(jnp.dot(xcb[:, cs], wx_ref[0, t], preferred_element_type=F32) + bx_ref[0, :, cs])
        log_a = (-LRU_C) * r * sp[:, cs]
        a = jnp.exp(log_a)
        a_ref[:, cs] = a
        b_ref[:, cs] = jnp.sqrt((1.0 - a) * (1.0 + a)) * (i * xc[:, cs])

    def body(t, h):
        tt = t + dr * (steps - 1 - 2 * t)
        rs = pl.ds(pl.multiple_of(tt * SUBLANES, SUBLANES), SUBLANES)
        h = a_ref[rs, :] * h + b_ref[rs, :]
        y_ref[0, 0, rs, :] = h
        return h

    h = lax.fori_loop(0, steps, body, hc_ref[...], unroll=4)
    hc_ref[...] = h
    fin_ref[0, 0] = h


def _lru_scan(x_t, conv_w, conv_b, wa_bd, wx_bd, ba, bx, lam, h0, steps_total):
    ng, rows_total, w = x_t.shape
    ntile, tw = wa_bd.shape[1], wa_bd.shape[2]
    steps = _tile(steps_total, SCAN_STEPS)
    assert steps % 2 == 0
    nt = steps_total // steps
    rows = steps * SUBLANES
    halo = 2 * SUBLANES
    cwid = conv_w.shape[0]

    def tb(dr, j):
        return j + dr * (nt - 1 - 2 * j)

    return pl.pallas_call(
        functools.partial(_lru_kernel, steps=steps, nt=nt, ntile=ntile, tw=tw),
        grid=(ng, 2, nt),
        in_specs=[pl.BlockSpec((1, rows, w), lambda g, dr, j: (g, tb(dr, j), 0)),
                  pl.BlockSpec((1, halo, w), lambda g, dr, j: (g, jnp.maximum(tb(dr, j) * (rows // halo) - 1, 0), 0)),
                  pl.BlockSpec((1, SUBLANES, w), lambda g, dr, j: (g, jnp.minimum((tb(dr, j) + 1) * steps, nt * steps - 1), 0)),
                  pl.BlockSpec((cwid, w), lambda g, dr, j: (0, 0)),
                  pl.BlockSpec((1, w), lambda g, dr, j: (0, 0)),
                  pl.BlockSpec((1, ntile, tw, tw), lambda g, dr, j: (dr, 0, 0, 0)),
                  pl.BlockSpec((1, ntile, tw, tw), lambda g, dr, j: (dr, 0, 0, 0)),
                  pl.BlockSpec((1, 1, w), lambda g, dr, j: (dr, 0, 0)),
                  pl.BlockSpec((1, 1, w), lambda g, dr, j: (dr, 0, 0)),
                  pl.BlockSpec((1, 1, w), lambda g, dr, j: (dr, 0, 0)),
                  pl.BlockSpec((1, 1, SUBLANES, w), lambda g, dr, j: (g, dr, 0, 0))],
        out_specs=[pl.BlockSpec((1, 1, rows, w), lambda g, dr, j: (dr, g, tb(dr, j), 0)),
                   pl.BlockSpec((1, 1, SUBLANES, w), lambda g, dr, j: (g, dr, 0, 0))],
        out_shape=[jax.ShapeDtypeStruct((2, ng, rows_total, w), F32),
                   jax.ShapeDtypeStruct((ng, 2, SUBLANES, w), F32)],
        scratch_shapes=[pltpu.VMEM((rows, w), F32), pltpu.VMEM((rows, w), F32), pltpu.VMEM((SUBLANES, w), F32)],
        compiler_params=_cparams(("arbitrary", "arbitrary", "arbitrary")),
    )(x_t, x_t, x_t, conv_w, conv_b.reshape(1, w), wa_bd, wx_bd, ba.reshape(2, 1, w), bx.reshape(2, 1, w),
      lam.reshape(2, 1, w), h0)


def _split3(x):
    x1 = x.astype(BF16)
    r1 = x - x1.astype(F32)
    x2 = r1.astype(BF16)
    x3 = (r1 - x2.astype(F32)).astype(BF16)
    return x1, x2, x3


def _tri_sums(tri_bf, parts):
    return sum(jnp.dot(tri_bf, p, preferred_element_type=F32) for p in parts)


def _log_sigmoid(x):
    return jnp.minimum(x, 0.0) - jnp.log1p(jnp.exp(-jnp.abs(x)))


def _mlstm_kernel(*refs, length, chunk, has_state, want_final, aliased, scale):
    q_ref, k_ref, v_ref, o_ref, g_ref, bg_ref, ng_ref = refs[:7]
    pos = 7
    if has_state:
        c0_ref, n0_ref, m0_ref = refs[pos:pos + 3]
        pos += 3
    if aliased:
        pos += 1
    y_ref = refs[pos]
    pos += 1
    if want_final:
        cf_ref, nf_ref, mf_ref = refs[pos:pos + 3]
        pos += 3
    hsum_ref, c_scr, n_scr = refs[pos:pos + 3]

    nc = length // chunk
    t_ = chunk
    row_i = lax.broadcasted_iota(jnp.int32, (t_, t_), 0)
    col_i = lax.broadcasted_iota(jnp.int32, (t_, t_), 1)
    lower = col_i <= row_i
    upper = col_i >= row_i
    lower_bf = lower.astype(BF16)
    upper_bf = upper.astype(BF16)

    gates = g_ref[...] + bg_ref[...]
    lane = lax.broadcasted_iota(jnp.int32, gates.shape, 1)
    gm = jnp.where((lane % 2) == 1, _log_sigmoid(gates), gates)

    def sweep(c, backward, state, need_update):
        c_st, n_st, m_st = state
        rs = slice(c * t_, (c + 1) * t_)
        gc = gm[rs]
        parts = _split3(gc)
        cols = _tri_sums(upper_bf if backward else lower_bf, parts)
        rows_ = cols.T
        gct = gc.T
        ci, cf = (2, 3) if backward else (0, 1)
        bc_col = cols[:, cf:cf + 1]
        bc_row = rows_[cf:cf + 1, :]
        i_col = gc[:, ci:ci + 1]
        i_row = gct[ci:ci + 1, :]
        b_last = bc_col[0:1] if backward else bc_col[t_ - 1:t_]
        mask = upper if backward else lower

        qs = (q_ref[rs, :] * scale)
        qb = qs.astype(BF16)
        kf = k_ref[rs, :]
        kb = kf.astype(BF16)
        vf = v_ref[rs, :]
        s = lax.dot_general(qb, kb, (((1,), (1,)), ((), ())), preferred_element_type=F32)
        dm = jnp.where(mask, bc_col - bc_row + i_row, NEG_INF)
        inter = bc_col + m_st
        m_t = jnp.maximum(inter, jnp.max(dm, axis=1, keepdims=True))
        sc = s * jnp.exp(dm - m_t)
        num = jnp.dot(sc.astype(BF16), vf.astype(BF16), preferred_element_type=F32)
        den = jnp.sum(sc, axis=1, keepdims=True)
        if c_st is not None:
            w_inter = jnp.exp(inter - m_t)
            num = num + w_inter * lax.dot_general(qb, c_st.astype(BF16), (((1,), (1,)), ((), ())),
                                                  preferred_element_type=F32)
            den = den + w_inter * jnp.sum(qs * n_st, axis=1, keepdims=True)
        h = num / jnp.maximum(jnp.abs(den), jnp.exp(-m_t))
        if not need_update:
            return h, None

        g_col = b_last - bc_col + i_col
        m_new = jnp.maximum(b_last + m_st, jnp.max(g_col, axis=0, keepdims=True))
        wg = jnp.exp(g_col - m_new)
        c_new = lax.dot_general((vf * wg).astype(BF16), kb, (((0,), (0,)), ((), ())), preferred_element_type=F32)
        n_new = jnp.sum(wg * kf, axis=0, keepdims=True)
        if c_st is not None:
            decay = jnp.exp(b_last + m_st - m_new)
            c_new = c_new + decay * c_st
            n_new = n_new + decay * n_st
        return h, (c_new, n_new, m_new)

    m_fin = []
    for d in range(2):
        backward = d == 1
        if has_state:
            state = (c0_ref[0, d, 0], n0_ref[0, d, 0], m0_ref[0, 0, :, d:d + 1])
        else:
            state = (None, None, jnp.zeros((1, 1), F32))
        order = range(nc - 1, -1, -1) if backward else range(nc)
        for idx, c in enumerate(order):
            if idx > 0:
                state = (c_scr[...], n_scr[...], state[2])
            h, state = sweep(c, backward, state, want_final or idx < nc - 1)
            rs = slice(c * t_, (c + 1) * t_)
            if d == 0:
                hsum_ref[rs, :] = h
            else:
                hsum_ref[rs, :] += h
            if idx < nc - 1:
                c_scr[...] = state[0]
                n_scr[...] = state[1]
        if want_final:
            cf_ref[0, d, 0] = state[0]
            nf_ref[0, d, 0] = state[1]
            m_fin.append(state[2])
    if want_final:
        lane_f = lax.broadcasted_iota(jnp.int32, (1, LANES), 1)
        mf_ref[0, 0] = jnp.where(lane_f == 0, m_fin[0], jnp.where(lane_f == 1, m_fin[1], 0.0))

    hs = hsum_ref[...]
    hn = hs * lax.rsqrt(jnp.mean(hs * hs, axis=1, keepdims=True) + EPS) * ng_ref[...]
    y_ref[...] = (_sigmoid(o_ref[...]) * hn).astype(y_ref.dtype)


def _mlstm(z, zg, bg, norm_g, offs, row0, nseq, length, nh, dv, dqk, state, want_final,
           layer=0, depth=1, c_buf=None):
    chunk = _tile(length, ML_CHUNK)
    has_state = state is not None
    rb0 = row0 // length
    qo, ko, vo, oo = (offs["q"] // dqk, offs["k"] // dqk, offs["v"] // dv, offs["o"] // dv)
    gw = zg.shape[1] // nh
    in_specs = [pl.BlockSpec((length, dqk), lambda n, h: (rb0 + n, qo + h)),
                pl.BlockSpec((length, dqk), lambda n, h: (rb0 + n, ko + h)),
                pl.BlockSpec((length, dv), lambda n, h: (rb0 + n, vo + h)),
                pl.BlockSpec((length, dv), lambda n, h: (rb0 + n, oo + h)),
                pl.BlockSpec((length, gw), lambda n, h: (rb0 + n, h)),
                pl.BlockSpec((1, gw), lambda n, h: (0, h)),
                pl.BlockSpec((1, dv), lambda n, h: (0, h))]
    args = [z, z, z, z, zg, bg, norm_g]
    if has_state:
        c0, n0, m0 = state
        in_specs += [pl.BlockSpec((1, 2, 1, dv, dqk), lambda n, h: (n, 0, h, 0, 0)),
                     pl.BlockSpec((1, 2, 1, 1, dqk), lambda n, h: (n, 0, h, 0, 0)),
                     pl.BlockSpec((1, 1, 1, LANES), lambda n, h: (n, h, 0, 0))]
        args += [c0, n0, m0]
    aliases = {}
    if c_buf is not None:
        aliases = {len(args): 1}
        in_specs.append(pl.BlockSpec(memory_space=pl.ANY))
        args.append(c_buf)
    out_specs = [pl.BlockSpec((length, dv), lambda n, h: (n, h))]
    out_shape = [jax.ShapeDtypeStruct((nseq * length, nh * dv), BF16)]
    if want_final:
        out_specs += [pl.BlockSpec((1, None, 2, 1, dv, dqk), lambda n, h: (n, layer, 0, h, 0, 0)),
                      pl.BlockSpec((1, 2, 1, 1, dqk), lambda n, h: (n, 0, h, 0, 0)),
                      pl.BlockSpec((1, 1, 1, LANES), lambda n, h: (n, h, 0, 0))]
        out_shape += [jax.ShapeDtypeStruct((nseq, depth, 2, nh, dv, dqk), F32),
                      jax.ShapeDtypeStruct((nseq, 2, nh, 1, dqk), F32),
                      jax.ShapeDtypeStruct((nseq, nh, 1, LANES), F32)]
    return pl.pallas_call(
        functools.partial(_mlstm_kernel, length=length, chunk=chunk, has_state=has_state,
                          want_final=want_final, aliased=c_buf is not None, scale=float(dqk) ** -0.5),
        grid=(nseq, nh),
        in_specs=in_specs,
        out_specs=out_specs,
        out_shape=out_shape,
        input_output_aliases=aliases,
        scratch_shapes=[pltpu.VMEM((length, dv), F32), pltpu.VMEM((dv, dqk), F32), pltpu.VMEM((1, dqk), F32)],
        compiler_params=_cparams(("arbitrary", "arbitrary")),
    )(*args)


def _mix_kernel(s5c_ref, s5l_ref, mlc_ref, mll_ref, lruc_ref, lrul_ref, gb_ref, o_ref, *, nct, w_s5, w_ml):
    is_ctx = pl.program_id(0) < nct
    o_ref[:, :w_s5] = jnp.where(is_ctx, s5c_ref[...], s5l_ref[...]).astype(o_ref.dtype)
    o_ref[:, w_s5:w_s5 + w_ml] = jnp.where(is_ctx, mlc_ref[...], mll_ref[...]).astype(o_ref.dtype)
    lru = jnp.where(is_ctx, lruc_ref[0] + lruc_ref[1], lrul_ref[0] + lrul_ref[1])
    o_ref[:, w_s5 + w_ml:] = (lru * _gelu_tanh(gb_ref[...])).astype(o_ref.dtype)


def _mix(y_s5, y_ml, y_lru, zb, gb_off):
    m_ctx, w_s5 = y_s5[0].shape
    m_lat = y_s5[1].shape[0]
    w_ml = y_ml[0].shape[1]
    w_lru = y_lru[0].shape[2]
    tm = _tile(_gcd(m_ctx, m_lat), 512)
    nct = m_ctx // tm
    gbo = gb_off // w_lru

    def ctx(i):
        return jnp.minimum(i, nct - 1)

    def lat(i):
        return jnp.maximum(i - nct, 0)

    return pl.pallas_call(
        functools.partial(_mix_kernel, nct=nct, w_s5=w_s5, w_ml=w_ml),
        grid=((m_ctx + m_lat) // tm,),
        in_specs=[pl.BlockSpec((tm, w_s5), lambda i: (ctx(i), 0)),
                  pl.BlockSpec((tm, w_s5), lambda i: (lat(i), 0)),
                  pl.BlockSpec((tm, w_ml), lambda i: (ctx(i), 0)),
                  pl.BlockSpec((tm, w_ml), lambda i: (lat(i), 0)),
                  pl.BlockSpec((2, tm, w_lru), lambda i: (0, ctx(i), 0)),
                  pl.BlockSpec((2, tm, w_lru), lambda i: (0, lat(i), 0)),
                  pl.BlockSpec((tm, w_lru), lambda i: (i, gbo))],
        out_specs=pl.BlockSpec((tm, w_s5 + w_ml + w_lru), lambda i: (i, 0)),
        out_shape=jax.ShapeDtypeStruct((m_ctx + m_lat, w_s5 + w_ml + w_lru), BF16),
        compiler_params=_cparams(("arbitrary",)),
    )(y_s5[0], y_s5[1], y_ml[0], y_ml[1], y_lru[0], y_lru[1], zb)


def _s5_params(lam_re, lam_im, log_dt, b_re, b_im, c_re, c_im):
    _, g, p = lam_re.shape
    h = b_re.shape[-1]
    cbw = min(MXU_DIM, g * h)
    gpc = cbw // h
    ncb = g // gpc
    dt = jnp.exp(log_dt)[..., None]
    mag = jnp.exp(lam_re * dt)
    lbr = mag * jnp.cos(lam_im * dt)
    lbi = mag * jnp.sin(lam_im * dt)
    den = lam_re * lam_re + lam_im * lam_im
    cr = (((lbr - 1.0) * lam_re + lbi * lam_im) / den)[..., None]
    ci = ((lbi * lam_re - (lbr - 1.0) * lam_im) / den)[..., None]
    bbr = cr * b_re - ci * b_im
    bbi = cr * b_im + ci * b_re
    eye = jnp.eye(gpc, dtype=F32)

    def expand_b(t):
        t = t.reshape(2, ncb, gpc, p, h)
        return jnp.einsum("dcgph,gk->dcghkp", t, eye).reshape(2, ncb, gpc * h, gpc * p)

    def expand_c(t):
        t = t.reshape(2, ncb, gpc, h, p)
        return jnp.einsum("dcghp,gk->dckpgh", t, eye).reshape(2, ncb, gpc * p, gpc * h)

    bd = jnp.concatenate([expand_b(bbr), expand_b(bbi)], axis=-1).astype(BF16)
    cd = jnp.concatenate([expand_c(c_re), -expand_c(c_im)], axis=2).astype(BF16)
    lr = jnp.broadcast_to(lbr.reshape(2, 1, g * p), (2, SUBLANES, g * p))
    li = jnp.broadcast_to(lbi.reshape(2, 1, g * p), (2, SUBLANES, g * p))
    return bd, cd, lr, li, (ncb, gpc)


def _s5_state_to_lanes(st, ncb, gpc):
    b, _, g, p, _ = st.shape
    t = st.reshape(b, 2, ncb, gpc, p, 2).transpose(1, 0, 2, 5, 3, 4)
    return t.reshape(2, b, 2 * g * p)


def _s5_lanes_to_state(fin, ncb, gpc, p):
    ng = fin.shape[0]
    t = fin.reshape(ng, 2, SUBLANES, ncb, 2, gpc, p).transpose(0, 2, 1, 3, 5, 6, 4)
    return t.reshape(ng * SUBLANES, 2, ncb * gpc, p, 2)


def _block_diag_tiles(w):
    _, nb, bs, _ = w.shape
    tw = min(MXU_DIM, nb * bs)
    bpt = tw // bs
    ntile = nb // bpt
    eye = jnp.eye(bpt, dtype=F32)
    t = jnp.einsum("dtbio,bk->dtbiko", w.reshape(2, ntile, bpt, bs, bs), eye)
    return t.reshape(2, ntile, tw, tw).astype(BF16)


def _to_scan_rows(t, nseq, length):
    c = t.shape[-1]
    t = t.reshape(nseq, length, c)
    pad = (-nseq) % SUBLANES
    if pad:
        t = jnp.pad(t, ((0, pad), (0, 0), (0, 0)))
    ng = (nseq + pad) // SUBLANES
    return t.reshape(ng, SUBLANES, length, c).transpose(0, 2, 1, 3).reshape(ng, length * SUBLANES, c)


def _from_scan_rows(t, nseq, length):
    lead = t.shape[:-3]
    ng, _, c = t.shape[-3:]
    t = t.reshape(lead + (ng, length, SUBLANES, c))
    t = jnp.moveaxis(t, -2, -3).reshape(lead + (ng * SUBLANES, length, c))
    return t[..., :nseq, :, :].reshape(lead + (nseq * length, c))


def _raster_to_columns(t, nseq, length):
    c = t.shape[-1]
    rows = length // GRID_W
    return t.reshape(nseq, rows, GRID_W, c).transpose(0, 2, 1, 3).reshape(nseq * length, c)


def _columns_to_raster(t, nseq, length):
    lead = t.shape[:-2]
    c = t.shape[-1]
    rows = length // GRID_W
    t = t.reshape(lead + (nseq, GRID_W, rows, c))
    return jnp.swapaxes(t, -2, -3).reshape(lead + (nseq * length, c))


def kernel(x_prompt, x_sample, c, state_s5, state_mlstm_c, state_mlstm_n, state_mlstm_m, state_lru, c_ctx, w_ada, b_ada, norm_gains, w_in, w_out, s5_lam_re, s5_lam_im, s5_log_dt, s5_b_re, s5_b_im, s5_c_re, s5_c_im, s5_d, s5_w_glu, s5_b_glu, ml_b_gate, ml_norm, lru_conv_w, lru_conv_b, lru_wa, lru_ba, lru_wx, lru_bx, lru_lam, w_ff1, w_ff2):
    bc, lc, d = x_prompt.shape
    bl, ll, _ = x_sample.shape
    depth = w_ada.shape[0]
    _, _, s5_g, s5_p = s5_lam_re.shape
    s5_h = s5_b_re.shape[-1]
    s5_w = s5_g * s5_h
    nh, dv, dqk = state_mlstm_c.shape[3:]
    ml_w, qk_w = nh * dv, nh * dqk
    lru_w = lru_lam.shape[-1]
    m_ctx, m_lat = bc * lc, bl * ll
    assert 1 + bl <= SUBLANES and bc % SUBLANES == 0

    sizes = (s5_w, qk_w, qk_w, ml_w, ml_w, 4 * nh, lru_w, lru_w)
    starts = [0]
    for s in sizes:
        starts.append(starts[-1] + s)
    offs = {"u": 0, "q": s5_w, "k": s5_w + qk_w, "v": s5_w + 2 * qk_w, "o": s5_w + 2 * qk_w + ml_w,
            "xb": 0, "gb": lru_w}
    nseg = SUBLANES // bl
    seglen = ll // nseg
    assert m_ctx % ll == 0 and bl * nseg == SUBLANES and ll % nseg == 0

    cond = jnp.zeros((SUBLANES, d), F32).at[0].set(c_ctx).at[1:1 + bl].set(c)
    mod = _ada_mod(cond, w_ada, b_ada).reshape(depth, SUBLANES, 6, d)

    x = jnp.concatenate([x_prompt.reshape(m_ctx, d), x_sample.reshape(m_lat, d)], axis=0)
    h1 = _prenorm(x, mod[0], norm_gains[0, 0], m_ctx, ll, 0, 1)

    w_in_bf = w_in.astype(BF16)
    w_b = w_in_bf[:, :, starts[6]:]
    w_gate = w_in_bf[:, :, starts[5]:starts[6]].reshape(depth, d, 4, nh).transpose(0, 1, 3, 2)
    w_gate = jnp.pad(w_gate, ((0, 0), (0, 0), (0, 0), (0, LANES - 4))).reshape(depth, d, nh * LANES)
    w_out_bf = w_out.astype(BF16)
    w_ff1_bf = w_ff1.astype(BF16)
    w_ff2_bf = w_ff2.astype(BF16)
    w_glu_bf = s5_w_glu.astype(BF16)

    finals = {k: [] for k in ("s5", "mn", "mm", "lru")}
    mc_buf = None
    for l in range(depth):
        bg = jnp.zeros((nh, LANES), F32).at[:, :4].set(ml_b_gate[l].T).reshape(1, nh * LANES)

        z = _matmul(h1, w_in_bf, l, n=starts[5])
        zb = _matmul(h1, w_b, l)
        zg = _matmul(h1, w_gate, l)

        y_ml_c, mc_buf, mn_f, mm_f = _mlstm(z, zg, bg, ml_norm[l].reshape(1, ml_w), offs, 0, bc, lc,
                                            nh, dv, dqk, None, True, layer=l, depth=depth, c_buf=mc_buf)
        m0 = jnp.zeros((bl, nh, 1, LANES), F32).at[:, :, 0, :2].set(state_mlstm_m[:, l].transpose(0, 2, 1))
        lat_state = (state_mlstm_c[:, l], state_mlstm_n[:, l].reshape(bl, 2, nh, 1, dqk), m0)
        (y_ml_l,) = _mlstm(z, zg, bg, ml_norm[l].reshape(1, ml_w), offs, m_ctx, bl, ll,
                           nh, dv, dqk, lat_state, False)
        finals["mn"].append(mn_f.reshape(bc, 2, nh, dqk))
        finals["mm"].append(mm_f[:, :, 0, :2].transpose(0, 2, 1))

        bd, cd, lr, li, (ncb, gpc) = _s5_params(s5_lam_re[l], s5_lam_im[l], s5_log_dt[l], s5_b_re[l], s5_b_im[l],
                                                s5_c_re[l], s5_c_im[l])
        u = z[:, offs["u"]:offs["u"] + s5_w]
        u_c = _to_scan_rows(u[:m_ctx], bc, lc)
        u_l = _to_scan_rows(u[m_ctx:], bl * nseg, seglen)
        lanes = 2 * s5_g * s5_p
        h0_c = jnp.zeros((u_c.shape[0], 2, SUBLANES, lanes), F32)
        h0_t = _s5_state_to_lanes(state_s5[:, l], ncb, gpc)
        h0_l = jnp.zeros((2, bl, nseg, lanes), F32).at[0, :, 0].set(h0_t[0]).at[1, :, nseg - 1].set(h0_t[1])
        h0_l = h0_l.reshape(2, SUBLANES, lanes)
        yd_c, fin_c = _s5_scan(u_c, bd, cd, lr, li, h0_c, lc)
        _, fin_l = _s5_scan(u_l, bd, cd, lr, li, h0_l[None], seglen, want_y=False)
        h0_l = _s5_segment_starts(fin_l[0], h0_l, lr, li, bl, nseg, seglen, ncb, s5_g * s5_p // ncb)
        yd_l, _ = _s5_scan(u_l, bd, cd, lr, li, h0_l[None], seglen)
        finals["s5"].append(_s5_lanes_to_state(fin_c, ncb, gpc, s5_p))
        d_row = s5_d[l].reshape(1, s5_w)
        wg_bf = w_glu_bf[l]
        bgl = s5_b_glu[l].reshape(1, s5_w)
        ys_c = _s5_post(yd_c.reshape(2, -1, s5_w), u_c.reshape(-1, s5_w), d_row, wg_bf, bgl)
        ys_l = _s5_post(yd_l.reshape(2, -1, s5_w), u_l.reshape(-1, s5_w), d_row, wg_bf, bgl)
        y_s5 = (_from_scan_rows(ys_c.reshape(-1, lc * SUBLANES, s5_w), bc, lc),
                _from_scan_rows(ys_l.reshape(-1, seglen * SUBLANES, s5_w), bl * nseg, seglen))

        xb = zb[:, offs["xb"]:offs["xb"] + lru_w]
        x_c = _to_scan_rows(xb[:m_ctx], bc, lc)
        x_l = _to_scan_rows(_raster_to_columns(xb[m_ctx:], bl, ll), bl, ll)
        wa_bd = _block_diag_tiles(lru_wa[l])
        wx_bd = _block_diag_tiles(lru_wx[l])
        g0_c = jnp.zeros((x_c.shape[0], 2, SUBLANES, lru_w), F32)
        g0_l = jnp.pad(state_lru[:, l].transpose(1, 0, 2), ((0, 0), (0, SUBLANES - bl), (0, 0)))[None]
        lru_args = (lru_conv_w[l], lru_conv_b[l], wa_bd, wx_bd, lru_ba[l], lru_bx[l], lru_lam[l])
        hd_c, lfin_c = _lru_scan(x_c, *lru_args, g0_c, lc)
        hd_l, _ = _lru_scan(x_l, *lru_args, g0_l, ll)
        finals["lru"].append(lfin_c.transpose(0, 2, 1, 3).reshape(bc, 2, lru_w))
        y_lru = (_from_scan_rows(hd_c, bc, lc), _columns_to_raster(_from_scan_rows(hd_l, bl, ll), bl, ll))

        mixcat = _mix(y_s5, (y_ml_c, y_ml_l), y_lru, zb, offs["gb"])
        mix = _matmul(mixcat, w_out_bf, l, out_dtype=BF16)
        x, h2 = _resnorm(x, mix, mod[l], mod[l], norm_gains[l, 1], norm_gains[l, 2], m_ctx, ll, 2, 3, 4, True)
        act = _matmul(h2, w_ff1_bf, l, out_dtype=BF16, relu2=True)
        ff = _matmul(act, w_ff2_bf, l, out_dtype=BF16)
        if l < depth - 1:
            x, h1 = _resnorm(x, ff, mod[l], mod[l + 1], norm_gains[l, 3], norm_gains[l + 1, 0], m_ctx, ll, 5, 0, 1, True)
    last = (mod[depth - 1], mod[depth - 1], norm_gains[depth - 1, 3], norm_gains[depth - 1, 0], m_ctx, ll, 5, 0, 1, False)
    y_prompt = _resnorm(x, ff, *last, row0=0, nrows=m_ctx)[0].reshape(bc, lc, d)
    y_sample = _resnorm(x, ff, *last, row0=m_ctx, nrows=m_lat)[0].reshape(bl, ll, d)
    return (y_prompt, y_sample, jnp.stack(finals["s5"], 1), mc_buf, jnp.stack(finals["mn"], 1),
            jnp.stack(finals["mm"], 1), jnp.stack(finals["lru"], 1))
```
